```python
import jax, jax.numpy as jnp
from jax import lax
import numpy as np

D_MODEL = 1024
BATCH = 4
SEQ = 4096
DEPTH = 2
DEC_BATCH = 32
DEC_SEQ = 64
PAST_LEN = 1024

CHUNK = 64
N_MIXERS = 2
N_GLA_LAYERS = (DEPTH + 1) // 2
N_CONV_LAYERS = DEPTH // 2
GLA_HEADS = 4
GLA_DK = D_MODEL // 2
GLA_DV = D_MODEL
GLA_DK_HEAD = GLA_DK // GLA_HEADS
GLA_DV_HEAD = GLA_DV // GLA_HEADS
GATE_RANK = 16
GATE_TAU = 16.0
GLA_IN_WIDTH = 2 * GLA_DK + 2 * GLA_DV + GATE_RANK
CONV_WIDTH = 31
N_EXPERTS = 32
TOP_K = 4
D_FF_EXPERT = D_MODEL
SWIGLU_LIMIT = 7.0
SWIGLU_ALPHA = 1.702
MOE_BLOCK = 128
EPS = 1e-6

kernel_name = "streaming_gla_conformer_moe_step"


def rms_norm(x, g):
    xf = x.astype(jnp.float32)
    y = xf * lax.rsqrt(jnp.mean(xf * xf, axis=-1, keepdims=True) + EPS)
    return (y * g.astype(jnp.float32)).astype(x.dtype)


def layer_norm(x, g, b):
    xf = x.astype(jnp.float32)
    xc = xf - jnp.mean(xf, axis=-1, keepdims=True)
    y = xc * lax.rsqrt(jnp.mean(xc * xc, axis=-1, keepdims=True) + EPS)
    return (y * g.astype(jnp.float32) + b.astype(jnp.float32)).astype(x.dtype)


def gla_chunk(S, xs):
    q, k, v, lg = xs
    C = q.shape[1]
    b = jnp.cumsum(lg, axis=1)
    causal = jnp.tril(jnp.ones((C, C), dtype=bool))[None, :, :, None, None]
    decay = jnp.exp(jnp.where(causal, b[:, :, None] - b[:, None, :], -jnp.inf))
    scores = jnp.einsum('bthk,bshk,btshk->bhts', q, k, decay)
    o = (jnp.einsum('bhts,bshv->bthv', scores, v)
         + jnp.einsum('bthk,bhkv->bthv', q * jnp.exp(b), S))
    b_last = b[:, -1]
    S_new = (jnp.exp(b_last)[..., None] * S
             + jnp.einsum('bshk,bshv->bhkv', k * jnp.exp(b_last[:, None] - b), v))
    return S_new, o


def gla_mixer(h, S0, w_in, w_gate_up, b_gate, g_norm, w_out):
    B, L, _ = h.shape
    f32 = jnp.float32
    q, k, v, r, gz = jnp.split(h @ w_in, [GLA_DK, 2 * GLA_DK, 2 * GLA_DK + GLA_DV, 2 * GLA_DK + 2 * GLA_DV], axis=-1)
    lg = jax.nn.log_sigmoid((gz @ w_gate_up + b_gate).astype(f32)) / GATE_TAU
    to_heads = lambda t, d: t.astype(f32).reshape(B, L, GLA_HEADS, d)
    q = to_heads(q, GLA_DK_HEAD) * GLA_DK_HEAD ** -0.5
    k = to_heads(k, GLA_DK_HEAD)
    lg = to_heads(lg, GLA_DK_HEAD)
    v = to_heads(v, GLA_DV_HEAD)
    C = min(L, CHUNK)
    n = L // C
    to_blocks = lambda t: jnp.moveaxis(t.reshape(B, n, C, GLA_HEADS, t.shape[-1]), 1, 0)
    S, o = lax.scan(gla_chunk, S0.astype(f32), (to_blocks(q), to_blocks(k), to_blocks(v), to_blocks(lg)))
    o = jnp.moveaxis(o, 0, 1).reshape(B, L, GLA_HEADS, GLA_DV_HEAD)
    o = rms_norm(o, g_norm).reshape(B, L, GLA_DV).astype(h.dtype) * jax.nn.silu(r)
    return o @ w_out, S.astype(h.dtype)


def conv_module(h, buf, w_pw1, b_pw1, w_dw, b_dw, ln_g, ln_b, w_pw2, b_pw2):
    a, g = jnp.split(h @ w_pw1 + b_pw1, 2, axis=-1)
    u = a * jax.nn.sigmoid(g)
    ext = jnp.concatenate([buf, u], axis=1)
    new_buf = ext[:, -(CONV_WIDTH - 1):]
    y = lax.conv_general_dilated(ext, w_dw[:, None, :], window_strides=(1,), padding='VALID',
                                 dimension_numbers=('NWC', 'WIO', 'NWC'),
                                 feature_group_count=D_MODEL) + b_dw
    y = jax.nn.silu(layer_norm(y, ln_g, ln_b))
    return y @ w_pw2 + b_pw2, new_buf


def moe(h, w_router, b_router, w1, b1, w2, b2):
    B, L, _ = h.shape
    xt = h.reshape(-1, D_MODEL)
    N = xt.shape[0]
    logits = (xt @ w_router + b_router).astype(jnp.float32)
    top_v, top_i = lax.top_k(logits, TOP_K)
    gates = jax.nn.softmax(top_v, axis=-1).astype(h.dtype)
    NK = N * TOP_K
    e_flat = top_i.reshape(-1)
    t_flat = jnp.arange(NK, dtype=jnp.int32) // TOP_K
    order = jnp.argsort(e_flat, stable=True)
    e_sorted = e_flat[order]
    counts = jnp.zeros((N_EXPERTS,), jnp.int32).at[e_flat].add(1)
    starts = jnp.cumsum(counts) - counts
    padded = (counts + MOE_BLOCK - 1) // MOE_BLOCK * MOE_BLOCK
    p_ends = jnp.cumsum(padded)
    p_starts = p_ends - padded
    dest = p_starts[e_sorted] + jnp.arange(NK, dtype=jnp.int32) - starts[e_sorted]
    n_blocks = -(-NK // MOE_BLOCK) + N_EXPERTS
    P = n_blocks * MOE_BLOCK
    row_tok = jnp.full((P,), N, jnp.int32).at[dest].set(t_flat[order])
    row_gate = jnp.zeros((P,), h.dtype).at[dest].set(gates.reshape(-1)[order])
    blk_e = jnp.clip(jnp.searchsorted(p_ends, jnp.arange(n_blocks, dtype=jnp.int32) * MOE_BLOCK, side='right'),
                     0, N_EXPERTS - 1)
    x_rows = jnp.concatenate([xt, jnp.zeros((1, D_MODEL), xt.dtype)], axis=0)[row_tok]
    x_rows = x_rows.reshape(n_blocks, MOE_BLOCK, D_MODEL)

    def expert_block(args):
        xb, e = args
        gate, up = jnp.split(xb @ w1[e] + b1[e], 2, axis=-1)
        gate = jnp.minimum(gate, SWIGLU_LIMIT)
        up = jnp.clip(up, -SWIGLU_LIMIT, SWIGLU_LIMIT)
        return ((up + 1) * (gate * jax.nn.sigmoid(SWIGLU_ALPHA * gate))) @ w2[e] + b2[e]

    y_rows = lax.map(expert_block, (x_rows, blk_e)).reshape(P, D_MODEL) * row_gate[:, None]
    y = jax.ops.segment_sum(y_rows, row_tok, num_segments=N + 1)[:N]
    return y.reshape(B, L, D_MODEL)


def trunk(x, c, gla_state, conv_buf, params):
    (w_ada, b_ada, norm_pre, norm_post, w_gla_in, w_gla_gate_up, b_gla_gate, g_gla_norm, w_gla_out,
     w_pw1, b_pw1, w_dw, b_dw, ln_g, ln_b, w_pw2, b_pw2,
     w_router, b_router, w_e1, b_e1, w_e2, b_e2) = params
    new_gla, new_conv = [], []
    cs = jax.nn.silu(c)
    for i in range(DEPTH):
        mod = (cs @ w_ada[i] + b_ada[i])[:, None, :]
        sh1, sc1, gt1, sh2, sc2, gt2 = jnp.split(mod, 6, axis=-1)
        h = rms_norm(x, norm_pre[i, 0]) * (1 + sc1) + sh1
        j = i // N_MIXERS
        if i % N_MIXERS == 0:
            mix, st = gla_mixer(h, gla_state[j], w_gla_in[j], w_gla_gate_up[j], b_gla_gate[j],
                                g_gla_norm[j], w_gla_out[j])
            new_gla.append(st)
        else:
            mix, st = conv_module(h, conv_buf[j], w_pw1[j], b_pw1[j], w_dw[j], b_dw[j],
                                  ln_g[j], ln_b[j], w_pw2[j], b_pw2[j])
            new_conv.append(st)
        x = x + gt1 * rms_norm(mix, norm_post[i, 0])
        h = rms_norm(x, norm_pre[i, 1]) * (1 + sc2) + sh2
        x = x + gt2 * rms_norm(moe(h, w_router[i], b_router[i], w_e1[i], b_e1[i], w_e2[i], b_e2[i]),
                               norm_post[i, 1])
    return x, jnp.stack(new_gla), jnp.stack(new_conv)


def setup_inputs(seed: int = 0) -> dict:
    key = jax.random.key(seed)
    ks = iter(jax.random.split(key, 32))

    def nrm(shape, scale):
        return jax.random.normal(next(ks), shape, jnp.float32) * scale

    def gain(shape):
        return 1.0 + nrm(shape, 0.05)

    D, E, F = D_MODEL, N_EXPERTS, D_FF_EXPERT
    return {
        "x_prompt": nrm((BATCH, SEQ, D), 1.0),
        "x_sample": nrm((DEC_BATCH, DEC_SEQ, D), 1.0),
        "state_gla": nrm((N_GLA_LAYERS, DEC_BATCH, GLA_HEADS, GLA_DK_HEAD, GLA_DV_HEAD), 1.0),
        "cache_conv": nrm((N_CONV_LAYERS, DEC_BATCH, CONV_WIDTH - 1, D), 0.5),
        "c_prompt": nrm((BATCH, D), 1.0),
        "c_sample": nrm((DEC_BATCH, D), 1.0),
        "w_ada": nrm((DEPTH, D, 6 * D), 0.5 * D ** -0.5),
        "b_ada": nrm((DEPTH, 6 * D), 0.02),
        "norm_pre": gain((DEPTH, 2, D)),
        "norm_post": gain((DEPTH, 2, D)),
        "w_gla_in": nrm((N_GLA_LAYERS, D, GLA_IN_WIDTH), D ** -0.5),
        "w_gla_gate_up": nrm((N_GLA_LAYERS, GATE_RANK, GLA_DK), GATE_RANK ** -0.5),
        "b_gla_gate": nrm((N_GLA_LAYERS, GLA_DK), 0.1),
        "g_gla_norm": gain((N_GLA_LAYERS, GLA_DV_HEAD)),
        "w_gla_out": nrm((N_GLA_LAYERS, GLA_DV, D), GLA_DV ** -0.5),
        "w_pw1": nrm((N_CONV_LAYERS, D, 2 * D), D ** -0.5),
        "b_pw1": nrm((N_CONV_LAYERS, 2 * D), 0.02),
        "w_dw": nrm((N_CONV_LAYERS, CONV_WIDTH, D), CONV_WIDTH ** -0.5),
        "b_dw": nrm((N_CONV_LAYERS, D), 0.02),
        "ln_g": gain((N_CONV_LAYERS, D)),
        "ln_b": nrm((N_CONV_LAYERS, D), 0.02),
        "w_pw2": nrm((N_CONV_LAYERS, D, D), D ** -0.5),
        "b_pw2": nrm((N_CONV_LAYERS, D), 0.02),
        "w_router": nrm((DEPTH, D, E), D ** -0.5),
        "b_router": nrm((DEPTH, E), 0.01),
        "w_e1": nrm((DEPTH, E, D, 2 * F), D ** -0.5),
        "b_e1": nrm((DEPTH, E, 2 * F), 0.02),
        "w_e2": nrm((DEPTH, E, F, D), F ** -0.5),
        "b_e2": nrm((DEPTH, E, D), 0.02),
    }


def reference(x_prompt, x_sample, state_gla, cache_conv, c_prompt, c_sample,
              w_ada, b_ada, norm_pre, norm_post,
              w_gla_in, w_gla_gate_up, b_gla_gate, g_gla_norm, w_gla_out,
              w_pw1, b_pw1, w_dw, b_dw, ln_g, ln_b, w_pw2, b_pw2,
              w_router, b_router, w_e1, b_e1, w_e2, b_e2):
    params = (w_ada, b_ada, norm_pre, norm_post, w_gla_in, w_gla_gate_up, b_gla_gate, g_gla_norm, w_gla_out,
              w_pw1, b_pw1, w_dw, b_dw, ln_g, ln_b, w_pw2, b_pw2,
              w_router, b_router, w_e1, b_e1, w_e2, b_e2)
    nb = x_prompt.shape[0]
    gla0 = jnp.zeros((N_GLA_LAYERS, nb, GLA_HEADS, GLA_DK_HEAD, GLA_DV_HEAD), x_prompt.dtype)
    conv0 = jnp.zeros((N_CONV_LAYERS, nb, CONV_WIDTH - 1, D_MODEL), x_prompt.dtype)
    y_prompt, gla_p, conv_p = trunk(x_prompt, c_prompt, gla0, conv0, params)
    y_sample, gla_s, conv_s = trunk(x_sample, c_sample, state_gla, cache_conv, params)
    return (y_prompt, y_sample, gla_p, gla_s, conv_p, conv_s)
```

```python
import functools

import numpy as np
import jax
import jax.numpy as jnp
from jax import lax
from jax.experimental import pallas as pl
from jax.experimental.pallas import tpu as pltpu

F32 = jnp.float32
BF16 = jnp.bfloat16

D_MODEL = 1024
CHUNK = 64
TILE_CHUNKS = 4
TILE = CHUNK * TILE_CHUNKS
GLA_HEADS = 4
DKH = 128
DVH = 256
SUB = 16
GATE_TAU = 16.0
CONV_WIDTH = 31
HIST = 32
N_EXPERTS = 32
TOP_K = 4
SWIGLU_LIMIT = 7.0
SWIGLU_ALPHA = 1.702
EXPERT_ROWS = 256
EPS = 1e-6
VMEM_LIMIT = 48 * 1024 * 1024


def _cparams(sem, vmem=VMEM_LIMIT):
    return pltpu.CompilerParams(dimension_semantics=sem, vmem_limit_bytes=vmem)


def _dot(a, b):
    return jnp.dot(a, b, preferred_element_type=F32)


def _dot_nt(a, b):
    return lax.dot_general(a, b, (((1,), (1,)), ((), ())), preferred_element_type=F32)


def _dot_tn(a, b):
    return lax.dot_general(a, b, (((0,), (0,)), ((), ())), preferred_element_type=F32)


def _sigmoid(x):
    return 1.0 / (1.0 + jnp.exp(-x))


def _rms(x, g):
    return x * lax.rsqrt(jnp.mean(x * x, axis=-1, keepdims=True) + EPS) * g


def _stack_rows(rows):
    n = rows[0].shape[1]
    ri = lax.broadcasted_iota(jnp.int32, (len(rows), n), 0)
    out = jnp.broadcast_to(rows[0], (len(rows), n))
    for j in range(1, len(rows)):
        out = jnp.where(ri == j, jnp.broadcast_to(rows[j], (len(rows), n)), out)
    return out


def _split3(x):
    hi = x.astype(BF16)
    r1 = x - hi.astype(F32)
    mid = r1.astype(BF16)
    lo = (r1 - mid.astype(F32)).astype(BF16)
    return hi, mid, lo


def _ada_kernel(c_ref, w_ref, b_ref, o_ref):
    c = c_ref[...]
    cs = (c * _sigmoid(c)).astype(BF16)
    o_ref[0] = _dot(cs, w_ref[0].astype(BF16)) + b_ref[0]


def _ada(c_all, w_ada, b_ada):
    depth, d, n6 = w_ada.shape
    rows = c_all.shape[0]
    nt = n6 // d
    return pl.pallas_call(
        _ada_kernel,
        grid=(depth, nt),
        in_specs=[
            pl.BlockSpec((rows, d), lambda i, n: (0, 0)),
            pl.BlockSpec((1, d, d), lambda i, n: (i, 0, n)),
            pl.BlockSpec((1, 1, d), lambda i, n: (i, 0, n)),
        ],
        out_specs=pl.BlockSpec((1, rows, d), lambda i, n: (i, 0, n)),
        out_shape=jax.ShapeDtypeStruct((depth, rows, n6), F32),
        compiler_params=_cparams(("arbitrary", "arbitrary")),
        name="ada",
    )(c_all, w_ada, b_ada.reshape(depth, 1, n6))


def _gla_in_kernel(x_ref, mod_ref, g_ref, w_ref, wgz_ref, wgu_ref, bg_ref,
                   q_ref, k_ref, v_ref, r_ref, lg_ref):
    m = mod_ref[...]
    h = _rms(x_ref[...], g_ref[...]) * (1.0 + m[:, 1:2, :]) + m[:, 0:1, :]
    hb = h.reshape(TILE, D_MODEL).astype(BF16)
    p = _dot(hb, w_ref[...])
    dk = GLA_HEADS * DKH
    dv = GLA_HEADS * DVH
    q_ref[...] = (p[:, :dk] * (DKH ** -0.5)).reshape(TILE_CHUNKS, CHUNK, dk)
    k_ref[...] = p[:, dk:2 * dk].reshape(TILE_CHUNKS, CHUNK, dk)
    v_ref[...] = p[:, 2 * dk:2 * dk + dv].reshape(TILE_CHUNKS, CHUNK, dv)
    r_ref[...] = p[:, 2 * dk + dv:].reshape(TILE_CHUNKS, CHUNK, dv)
    gz = _dot(hb, wgz_ref[...])
    z = _dot(gz.astype(BF16), wgu_ref[...]) + bg_ref[...]
    ls = jnp.minimum(z, 0.0) - jnp.log(1.0 + jnp.exp(-jnp.abs(z)))
    lg_ref[...] = (ls / GATE_TAU).reshape(TILE_CHUNKS, CHUNK, dk)


def _gla_in(x, modc, g_pre, w_main, w_gz, w_gu, b_gate):
    nc = x.shape[0]
    dk = GLA_HEADS * DKH
    dv = GLA_HEADS * DVH
    tok = lambda w: pl.BlockSpec((TILE_CHUNKS, CHUNK, w), lambda i: (i, 0, 0))
    full = lambda a: pl.BlockSpec(a.shape, lambda i: (0,) * a.ndim)
    outs = [jax.ShapeDtypeStruct((nc, CHUNK, w), F32) for w in (dk, dk, dv, dv, dk)]
    return pl.pallas_call(
        _gla_in_kernel,
        grid=(nc // TILE_CHUNKS,),
        in_specs=[tok(D_MODEL), pl.BlockSpec((TILE_CHUNKS, 6, D_MODEL), lambda i: (i, 0, 0)),
                  full(g_pre), full(w_main), full(w_gz), full(w_gu), full(b_gate)],
        out_specs=[tok(dk), tok(dk), tok(dv), tok(dv), tok(dk)],
        out_shape=outs,
        compiler_params=_cparams(("arbitrary",)),
        name="gla_in",
    )(x, modc, g_pre, w_main, w_gz, w_gu, b_gate)


def _gla_core_kernel(q_ref, k_ref, v_ref, lg_ref, s0_ref, o_ref, sout_ref, s_scr, *, bb, nchunks):
    j = pl.program_id(1)

    @pl.when(j == 0)
    def _():
        s_scr[...] = s0_ref[...]

    nsub = CHUNK // SUB
    rr = lax.broadcasted_iota(jnp.int32, (CHUNK, CHUNK), 0)
    cc = lax.broadcasted_iota(jnp.int32, (CHUNK, CHUNK), 1)
    tril = (cc <= rr).astype(BF16)
    ones_k = jnp.ones((DKH, DKH), BF16)
    pr = lax.broadcasted_iota(jnp.int32, (SUB * SUB, 2 * CHUNK), 0)
    pc = lax.broadcasted_iota(jnp.int32, (SUB * SUB, 2 * CHUNK), 1)
    pt, ps = pr // SUB, pr % SUB
    place = [((pc == SUB * i + ps) & (ps <= pt)).astype(F32) for i in range(nsub)]
    st = lax.broadcasted_iota(jnp.int32, (SUB, SUB * SUB), 0)
    sp = lax.broadcasted_iota(jnp.int32, (SUB, SUB * SUB), 1)
    sel = (sp // SUB == st).astype(BF16)
    key_row = lax.broadcasted_iota(jnp.int32, (CHUNK, DKH), 0)
    zeros_k = jnp.zeros((CHUNK, DKH), BF16)
    zeros_v = jnp.zeros((CHUNK, DVH), BF16)

    def per_seq(b, carry):
        lg = lg_ref[b]
        hi, mid, lo = _split3(lg)
        bcum = _dot(tril, hi) + _dot(tril, mid) + _dot(tril, lo)
        q_all = q_ref[b]
        k_all = k_ref[b]
        v_all = v_ref[b]
        outs = []
        for h in range(GLA_HEADS):
            qh = q_all[:, h * DKH:(h + 1) * DKH]
            kh = k_all[:, h * DKH:(h + 1) * DKH]
            vh = v_all[:, h * DVH:(h + 1) * DVH]
            bh = bcum[:, h * DKH:(h + 1) * DKH]
            vb = vh.astype(BF16)
            blast = bh[CHUNK - 1:CHUNK, :]
            s_t = s_scr[b, h]
            o = _dot_nt((qh * jnp.exp(bh)).astype(BF16), s_t.astype(BF16))
            kdec = (kh * jnp.exp(blast - bh)).astype(BF16)
            s_scr[b, h] = s_t * jnp.exp(blast) + _dot_tn(vb, kdec)
            a_rows = []
            for i in range(nsub):
                lo_r, hi_r = SUB * i, SUB * (i + 1)
                qi, ki, bi = qh[lo_r:hi_r], kh[lo_r:hi_r], bh[lo_r:hi_r]
                qe = jnp.concatenate([jnp.broadcast_to(qi[t:t + 1], (SUB, DKH)) for t in range(SUB)], axis=0)
                be = jnp.concatenate([jnp.broadcast_to(bi[t:t + 1], (SUB, DKH)) for t in range(SUB)], axis=0)
                kt = jnp.concatenate([ki] * SUB, axis=0)
                bt = jnp.concatenate([bi] * SUB, axis=0)
                pair = qe * kt * jnp.exp(jnp.minimum(be - bt, 0.0))
                rsum = _dot(pair.astype(BF16), ones_k)
                a_i = _dot(sel, (rsum * place[i]).astype(BF16))
                if i > 0:
                    anc = bh[lo_r - 1:lo_r, :]
                    qd = (qi * jnp.exp(bi - anc)).astype(BF16)
                    kd = kh * jnp.exp(jnp.minimum(anc - bh, 0.0))
                    kd = jnp.where(key_row < lo_r, kd, 0.0).astype(BF16)
                    a_i = a_i + _dot_nt(qd, jnp.concatenate([kd, zeros_k], axis=0))
                a_rows.append(a_i)
            a = jnp.concatenate(a_rows, axis=0).astype(BF16)
            o = o + _dot(a, jnp.concatenate([vb, zeros_v], axis=0))
            outs.append(o)
        o_ref[b] = jnp.concatenate(outs, axis=-1)
        return carry

    lax.fori_loop(0, bb, per_seq, 0)

    @pl.when(j == nchunks - 1)
    def _():
        sout_ref[...] = s_scr[...]


def _gla_core(q, k, v, lg, s0_t, *, chunk0, nseq, nchunks, bb):
    dk = GLA_HEADS * DKH
    dv = GLA_HEADS * DVH
    base = chunk0 // bb if nchunks == 1 else chunk0
    if nchunks == 1:
        cidx = lambda s, j: (base + s, 0, 0)
        oidx = lambda s, j: (s, 0, 0)
    else:
        assert bb == 1
        cidx = lambda s, j: (base + s * nchunks + j, 0, 0)
        oidx = lambda s, j: (s * nchunks + j, 0, 0)
    tok = lambda w: pl.BlockSpec((bb, CHUNK, w), cidx)
    st_spec = pl.BlockSpec((bb, GLA_HEADS, DVH, DKH), lambda s, j: (s, 0, 0, 0))
    return pl.pallas_call(
        functools.partial(_gla_core_kernel, bb=bb, nchunks=nchunks),
        grid=(nseq // bb, nchunks),
        in_specs=[tok(dk), tok(dk), tok(dv), tok(dk), st_spec],
        out_specs=[pl.BlockSpec((bb, CHUNK, dv), oidx), st_spec],
        out_shape=[jax.ShapeDtypeStruct((nseq * nchunks, CHUNK, dv), F32),
                   jax.ShapeDtypeStruct((nseq, GLA_HEADS, DVH, DKH), F32)],
        scratch_shapes=[pltpu.VMEM((bb, GLA_HEADS, DVH, DKH), F32)],
        compiler_params=_cparams(("arbitrary", "arbitrary")),
        name="gla_core",
    )(q, k, v, lg, s0_t)


def _gla_out_kernel(o_ref, r_ref, x_ref, mod_ref, gn_ref, w_ref, gpost_ref, x1_ref):
    o = o_ref[...].reshape(TILE, GLA_HEADS * DVH)
    r = r_ref[...].reshape(TILE, GLA_HEADS * DVH)
    gn = gn_ref[...]
    parts = [_rms(o[:, h * DVH:(h + 1) * DVH], gn) for h in range(GLA_HEADS)]
    y = jnp.concatenate(parts, axis=-1) * (r * _sigmoid(r))
    mix = _dot(y.astype(BF16), w_ref[...]).reshape(TILE_CHUNKS, CHUNK, D_MODEL)
    m = mod_ref[...]
    x1_ref[...] = x_ref[...] + m[:, 2:3, :] * _rms(mix, gpost_ref[...])


def _gla_out(o, r, x, modc, g_norm, w_out, g_post):
    nc = x.shape[0]
    tok = lambda w: pl.BlockSpec((TILE_CHUNKS, CHUNK, w), lambda i: (i, 0, 0))
    full = lambda a: pl.BlockSpec(a.shape, lambda i: (0,) * a.ndim)
    return pl.pallas_call(
        _gla_out_kernel,
        grid=(nc // TILE_CHUNKS,),
        in_specs=[tok(GLA_HEADS * DVH), tok(GLA_HEADS * DVH), tok(D_MODEL),
                  pl.BlockSpec((TILE_CHUNKS, 6, D_MODEL), lambda i: (i, 0, 0)),
                  full(g_norm), full(w_out), full(g_post)],
        out_specs=tok(D_MODEL),
        out_shape=jax.ShapeDtypeStruct(x.shape, F32),
        compiler_params=_cparams(("arbitrary",)),
        name="gla_out",
    )(o, r, x, modc, g_norm, w_out, g_post)


def _conv_kernel(x_ref, mod_ref, hist0_ref, gpre_ref, w1_ref, b1_ref, wdw_ref, bdw_ref, lng_ref, lnb_ref,
                 w2_ref, b2_ref, gpost_ref, x1_ref, hist_out_ref, ext_scr, *, nseq, ntiles):
    j = pl.program_id(1)
    tt = TILE // nseq

    @pl.when(j == 0)
    def _():
        for s in range(nseq):
            ext_scr[s, 0:HIST, :] = hist0_ref[s]

    m = mod_ref[...]
    h = _rms(x_ref[...], gpre_ref[...]) * (1.0 + m[:, 1:2, :]) + m[:, 0:1, :]
    p = _dot(h.reshape(TILE, D_MODEL).astype(BF16), w1_ref[...]) + b1_ref[...]
    u = p[:, :D_MODEL] * _sigmoid(p[:, D_MODEL:])
    first = HIST - (CONV_WIDTH - 1)
    ys = []
    for s in range(nseq):
        ext_scr[s, HIST:HIST + tt, :] = u[s * tt:(s + 1) * tt]
        cols = []
        for c in range(D_MODEL // 128):
            lanes = slice(c * 128, (c + 1) * 128)
            acc = jnp.zeros((tt, 128), F32)
            for tap in range(CONV_WIDTH):
                acc = acc + ext_scr[s, first + tap:first + tap + tt, lanes] * wdw_ref[tap:tap + 1, lanes]
            cols.append(acc)
        ys.append(jnp.concatenate(cols, axis=-1))
        hist_new = ext_scr[s, tt:tt + HIST, :]
        ext_scr[s, 0:HIST, :] = hist_new

        @pl.when(j == ntiles - 1)
        def _():
            hist_out_ref[s] = hist_new
    y = jnp.concatenate(ys, axis=0) + bdw_ref[...]
    yc = y - jnp.mean(y, axis=-1, keepdims=True)
    yn = yc * lax.rsqrt(jnp.mean(yc * yc, axis=-1, keepdims=True) + EPS) * lng_ref[...] + lnb_ref[...]
    act = yn * _sigmoid(yn)
    mix = (_dot(act.astype(BF16), w2_ref[...]) + b2_ref[...]).reshape(TILE_CHUNKS, CHUNK, D_MODEL)
    x1_ref[...] = x_ref[...] + m[:, 2:3, :] * _rms(mix, gpost_ref[...])


def _conv(x, modc, hist0, params, *, chunk0, nseq_total, ntiles, nseq):
    tile0 = chunk0 // TILE_CHUNKS
    if ntiles == 1:
        cidx = lambda s, j: (tile0 + s, 0, 0)
        oidx = lambda s, j: (s, 0, 0)
    else:
        assert nseq == 1
        cidx = lambda s, j: (tile0 + s * ntiles + j, 0, 0)
        oidx = lambda s, j: (s * ntiles + j, 0, 0)
    full = lambda a: pl.BlockSpec(a.shape, lambda s, j: (0,) * a.ndim)
    hist_spec = pl.BlockSpec((nseq, HIST, D_MODEL), lambda s, j: (s, 0, 0))
    nblocks = nseq_total // nseq
    return pl.pallas_call(
        functools.partial(_conv_kernel, nseq=nseq, ntiles=ntiles),
        grid=(nblocks, ntiles),
        in_specs=[pl.BlockSpec((TILE_CHUNKS, CHUNK, D_MODEL), cidx),
                  pl.BlockSpec((TILE_CHUNKS, 6, D_MODEL), cidx),
                  hist_spec] + [full(a) for a in params],
        out_specs=[pl.BlockSpec((TILE_CHUNKS, CHUNK, D_MODEL), oidx), hist_spec],
        out_shape=[jax.ShapeDtypeStruct((nblocks * ntiles * TILE_CHUNKS, CHUNK, D_MODEL), F32),
                   jax.ShapeDtypeStruct((nseq_total, HIST, D_MODEL), F32)],
        scratch_shapes=[pltpu.VMEM((nseq, HIST + TILE // nseq, D_MODEL), F32)],
        compiler_params=_cparams(("arbitrary", "arbitrary")),
        name="conv",
    )(x, modc, hist0, *params)


def _router_kernel(x_ref, mod_ref, g_ref, wrt_ref, br_ref,
                   h_ref, idx_ref, gate_ref, rank_ref, cnt_ref, carry_scr):
    i = pl.program_id(0)

    @pl.when(i == 0)
    def _():
        carry_scr[...] = jnp.zeros_like(carry_scr)

    m = mod_ref[...]
    h = _rms(x_ref[...], g_ref[...]) * (1.0 + m[:, 4:5, :]) + m[:, 3:4, :]
    h_ref[...] = h
    h2 = h.reshape(TILE, D_MODEL)
    h_hi = h2.astype(BF16)
    h_lo = (h2 - h_hi.astype(F32)).astype(BF16)
    w = wrt_ref[...]
    w_hi = w.astype(BF16)
    w_lo = (w - w_hi.astype(F32)).astype(BF16)
    logits = _dot_nt(w_hi, h_hi) + _dot_nt(w_lo, h_hi) + _dot_nt(w_hi, h_lo) + br_ref[...]
    e_iota = lax.broadcasted_iota(jnp.int32, (N_EXPERTS, TILE), 0)
    vals, idxs, sels = [], [], []
    lt = logits
    for _ in range(TOP_K):
        mx = jnp.max(lt, axis=0, keepdims=True)
        ix = jnp.min(jnp.where(lt == mx, e_iota, N_EXPERTS), axis=0, keepdims=True)
        sl = e_iota == ix
        vals.append(mx)
        idxs.append(ix)
        sels.append(sl)
        lt = jnp.where(sl, -jnp.inf, lt)
    ex = [jnp.exp(v - vals[0]) for v in vals]
    den = ex[0] + ex[1] + ex[2] + ex[3]
    gate_ref[...] = _stack_rows([e / den for e in ex])
    idx_ref[...] = _stack_rows(idxs)
    chosen = (sels[0] | sels[1] | sels[2] | sels[3])
    chosen_f = jnp.where(chosen, 1.0, 0.0)
    rr = lax.broadcasted_iota(jnp.int32, (TILE, TILE), 0)
    cc = lax.broadcasted_iota(jnp.int32, (TILE, TILE), 1)
    before = (rr < cc).astype(BF16)
    pos = carry_scr[...][:, 0:1] + _dot(chosen_f.astype(BF16), before)
    ranks = [jnp.sum(jnp.where(sl, pos, 0.0), axis=0, keepdims=True) for sl in sels]
    rank_ref[...] = _stack_rows(ranks).astype(jnp.int32)
    carry_scr[...] = carry_scr[...] + jnp.sum(chosen_f, axis=1, keepdims=True)
    cnt_ref[...] = carry_scr[...]


def _router(x, modc, g_pre, w_router_t, b_router):
    nc = x.shape[0]
    nt = nc // TILE_CHUNKS
    n = nc * CHUNK
    full = lambda a: pl.BlockSpec(a.shape, lambda i: (0,) * a.ndim)
    tok = pl.BlockSpec((TILE_CHUNKS, CHUNK, D_MODEL), lambda i: (i, 0, 0))
    kt = pl.BlockSpec((TOP_K, TILE), lambda i: (0, i))
    return pl.pallas_call(
        _router_kernel,
        grid=(nt,),
        in_specs=[tok, pl.BlockSpec((TILE_CHUNKS, 6, D_MODEL), lambda i: (i, 0, 0)),
                  full(g_pre), full(w_router_t), full(b_router)],
        out_specs=[tok, kt, kt, kt, pl.BlockSpec((N_EXPERTS, 128), lambda i: (0, 0))],
        out_shape=[jax.ShapeDtypeStruct(x.shape, F32),
                   jax.ShapeDtypeStruct((TOP_K, n), jnp.int32),
                   jax.ShapeDtypeStruct((TOP_K, n), F32),
                   jax.ShapeDtypeStruct((TOP_K, n), jnp.int32),
                   jax.ShapeDtypeStruct((N_EXPERTS, 128), F32)],
        scratch_shapes=[pltpu.VMEM((N_EXPERTS, 128), F32)],
        compiler_params=_cparams(("arbitrary",)),
        name="router",
    )(x, modc, g_pre, w_router_t, b_router)


def _row_copy(src, dst, sem):
    return pltpu.make_async_copy(src, dst, sem)


def _dispatch_kernel(dest_ref, h_ref, zero_ref, rows_ref, sem):
    del zero_ref
    base = pl.program_id(0) * (TOP_K * TILE)

    def issue(r, c):
        for k in range(TOP_K):
            d = dest_ref[base + k * TILE + r]
            _row_copy(h_ref.at[pl.ds(r, 1)], rows_ref.at[pl.ds(d, 1)], sem).start()
        return c

    lax.fori_loop(0, TILE, issue, 0, unroll=8)

    def drain(r, c):
        for k in range(TOP_K):
            _row_copy(h_ref.at[pl.ds(0, 1)], rows_ref.at[pl.ds(0, 1)], sem).wait()
        return c

    lax.fori_loop(0, TILE, drain, 0, unroll=8)


def _dispatch(dest_flat, h, n_rows):
    n = h.shape[0]
    zeros = jnp.zeros((n_rows, D_MODEL), F32)
    return pl.pallas_call(
        _dispatch_kernel,
        grid_spec=pltpu.PrefetchScalarGridSpec(
            num_scalar_prefetch=1,
            grid=(n // TILE,),
            in_specs=[pl.BlockSpec((TILE, D_MODEL), lambda i, dest: (i, 0)),
                      pl.BlockSpec(memory_space=pl.ANY)],
            out_specs=pl.BlockSpec(memory_space=pl.ANY),
            scratch_shapes=[pltpu.SemaphoreType.DMA],
        ),
        out_shape=jax.ShapeDtypeStruct((n_rows, D_MODEL), F32),
        input_output_aliases={2: 0},
        compiler_params=_cparams(("arbitrary",)),
        name="dispatch",
    )(dest_flat, h, zeros)


def _experts_kernel(be_ref, nu_ref, x_ref, w1_ref, b1_ref, w2_ref, b2_ref, y_ref, w1_scr, w2_scr):
    i = pl.program_id(0)
    valid = i < nu_ref[0]
    prev = be_ref[jnp.maximum(i - 1, 0)]
    changed = jnp.logical_or(i == 0, be_ref[i] != prev)
    dff = w2_scr.shape[0]

    @pl.when(jnp.logical_and(valid, changed))
    def _():
        rows = 128

        def cast(c, carry):
            r0 = pl.multiple_of(c * rows, rows)
            w1_scr[pl.ds(r0, rows), :] = w1_ref[0, pl.ds(r0, rows), :].astype(BF16)
            w2_scr[pl.ds(r0, rows), :] = w2_ref[0, pl.ds(r0, rows), :].astype(BF16)
            return carry

        lax.fori_loop(0, D_MODEL // rows, cast, 0)

    @pl.when(valid)
    def _():
        h = _dot(x_ref[...].astype(BF16), w1_scr[...]) + b1_ref[0]
        gate = jnp.minimum(h[:, :dff], SWIGLU_LIMIT)
        up = jnp.clip(h[:, dff:], -SWIGLU_LIMIT, SWIGLU_LIMIT)
        act = (up + 1.0) * (gate * _sigmoid(SWIGLU_ALPHA * gate))
        y_ref[...] = _dot(act.astype(BF16), w2_scr[...]) + b2_ref[0]

    @pl.when(jnp.logical_not(valid))
    def _():
        y_ref[...] = jnp.zeros_like(y_ref)


def _experts(blk_e, n_used, x_rows, w1, b1, w2, b2):
    n_rows = x_rows.shape[0]
    n_blocks = n_rows // EXPERT_ROWS
    e, d, f2 = w1.shape
    dff = w2.shape[1]
    assert d == dff == D_MODEL
    return pl.pallas_call(
        _experts_kernel,
        grid_spec=pltpu.PrefetchScalarGridSpec(
            num_scalar_prefetch=2,
            grid=(n_blocks,),
            in_specs=[pl.BlockSpec((EXPERT_ROWS, D_MODEL), lambda i, be, nu: (i, 0)),
                      pl.BlockSpec((1, d, f2), lambda i, be, nu: (be[i], 0, 0)),
                      pl.BlockSpec((1, 1, f2), lambda i, be, nu: (be[i], 0, 0)),
                      pl.BlockSpec((1, dff, D_MODEL), lambda i, be, nu: (be[i], 0, 0)),
                      pl.BlockSpec((1, 1, D_MODEL), lambda i, be, nu: (be[i], 0, 0))],
            out_specs=pl.BlockSpec((EXPERT_ROWS, D_MODEL), lambda i, be, nu: (i, 0)),
            scratch_shapes=[pltpu.VMEM((d, f2), BF16), pltpu.VMEM((dff, D_MODEL), BF16)],
        ),
        out_shape=jax.ShapeDtypeStruct((n_rows, D_MODEL), F32),
        compiler_params=_cparams(("arbitrary",)),
        name="experts",
    )(blk_e, n_used, x_rows, w1, b1.reshape(e, 1, f2), w2, b2.reshape(e, 1, D_MODEL))


def _combine_kernel(dest_ref, gate_ref, x_ref, mod_ref, gpost_ref, rows_ref, o_ref, ybuf, sem):
    base = pl.program_id(0) * (TOP_K * TILE)

    def issue(r, c):
        for k in range(TOP_K):
            d = dest_ref[base + k * TILE + r]
            _row_copy(rows_ref.at[pl.ds(d, 1)], ybuf.at[k, pl.ds(r, 1)], sem).start()
        return c

    lax.fori_loop(0, TILE, issue, 0, unroll=8)

    def drain(r, c):
        for k in range(TOP_K):
            _row_copy(rows_ref.at[pl.ds(0, 1)], ybuf.at[0, pl.ds(0, 1)], sem).wait()
        return c

    lax.fori_loop(0, TILE, drain, 0, unroll=8)

    g = gate_ref[...]
    y = g[:, 0:1] * ybuf[0]
    for k in range(1, TOP_K):
        y = y + g[:, k:k + 1] * ybuf[k]
    m = mod_ref[...]
    y = y.reshape(TILE_CHUNKS, CHUNK, D_MODEL)
    o_ref[...] = x_ref[...] + m[:, 5:6, :] * _rms(y, gpost_ref[...])


def _combine(dest_flat, gates, x, modc, g_post, y_rows):
    nc = x.shape[0]
    tok = pl.BlockSpec((TILE_CHUNKS, CHUNK, D_MODEL), lambda i, dest: (i, 0, 0))
    return pl.pallas_call(
        _combine_kernel,
        grid_spec=pltpu.PrefetchScalarGridSpec(
            num_scalar_prefetch=1,
            grid=(nc // TILE_CHUNKS,),
            in_specs=[pl.BlockSpec((TILE, TOP_K), lambda i, dest: (i, 0)),
                      tok,
                      pl.BlockSpec((TILE_CHUNKS, 6, D_MODEL), lambda i, dest: (i, 0, 0)),
                      pl.BlockSpec(g_post.shape, lambda i, dest: (0, 0)),
                      pl.BlockSpec(memory_space=pl.ANY)],
            out_specs=tok,
            scratch_shapes=[pltpu.VMEM((TOP_K, TILE, D_MODEL), F32), pltpu.SemaphoreType.DMA],
        ),
        out_shape=jax.ShapeDtypeStruct(x.shape, F32),
        compiler_params=_cparams(("arbitrary",)),
        name="combine",
    )(dest_flat, gates, x, modc, g_post, y_rows)


def _moe_layer(x, modc, g_pre, g_post, w_router, b_router, w1, b1, w2, b2):
    nc = x.shape[0]
    n = nc * CHUNK
    nt = n // TILE
    h, idx_t, gate_t, rank_t, cnt = _router(x, modc, g_pre, w_router.T, b_router.reshape(N_EXPERTS, 1))
    counts = cnt[:, 0].astype(jnp.int32)
    padded = (counts + EXPERT_ROWS - 1) // EXPERT_ROWS * EXPERT_ROWS
    p_ends = jnp.cumsum(padded)
    p_starts = p_ends - padded
    n_blocks = -(-(n * TOP_K) // EXPERT_ROWS) + N_EXPERTS
    n_used = (p_ends[-1] // EXPERT_ROWS).astype(jnp.int32)
    blk = jnp.minimum(jnp.arange(n_blocks, dtype=jnp.int32), n_used - 1) * EXPERT_ROWS
    blk_e = jnp.clip(jnp.searchsorted(p_ends, blk, side='right'), 0, N_EXPERTS - 1).astype(jnp.int32)
    dest_t = p_starts[idx_t] + rank_t
    dest_flat = dest_t.reshape(TOP_K, nt, TILE).transpose(1, 0, 2).reshape(-1)
    x_rows = _dispatch(dest_flat, h.reshape(n, D_MODEL), n_blocks * EXPERT_ROWS)
    y_rows = _experts(blk_e, n_used.reshape(1), x_rows, w1, b1, w2, b2)
    return _combine(dest_flat, gate_t.T, x, modc, g_post, y_rows)


def kernel(x_prompt, x_sample, state_gla, cache_conv, c_prompt, c_sample, w_ada, b_ada, norm_pre, norm_post, w_gla_in, w_gla_gate_up, b_gla_gate, g_gla_norm, w_gla_out, w_pw1, b_pw1, w_dw, b_dw, ln_g, ln_b, w_pw2, b_pw2, w_router, b_router, w_e1, b_e1, w_e2, b_e2):
    bp, lp, d = x_prompt.shape
    bs, ls, _ = x_sample.shape
    depth = w_ada.shape[0]
    assert d == D_MODEL and ls == CHUNK and lp % TILE == 0 and bs % TILE_CHUNKS == 0
    ncp = bp * lp // CHUNK
    ncs = bs
    nseq = bp + bs
    x = jnp.concatenate([x_prompt.reshape(ncp, CHUNK, d), x_sample.reshape(ncs, CHUNK, d)], axis=0)
    rows = -(-nseq // 8) * 8
    c_all = jnp.concatenate([c_prompt, c_sample, jnp.zeros((rows - nseq, d), F32)], axis=0)
    mod = _ada(c_all, w_ada, b_ada)
    mod_p = jnp.broadcast_to(mod[:, :bp, None, :], (depth, bp, lp // CHUNK, 6 * d)).reshape(depth, ncp, 6 * d)
    modc = jnp.concatenate([mod_p, mod[:, bp:nseq]], axis=1).reshape(depth, ncp + ncs, 6, d)

    row = lambda a: a.reshape(1, -1)
    dk = GLA_HEADS * DKH
    dv = GLA_HEADS * DVH
    gla_states, conv_hists = [], []
    for i in range(depth):
        j = i // 2
        if i % 2 == 0:
            w_in = w_gla_in[j]
            rank = w_gla_gate_up.shape[1]
            w_main = w_in[:, :2 * dk + 2 * dv].astype(BF16)
            w_gz = jnp.pad(w_in[:, 2 * dk + 2 * dv:], ((0, 0), (0, 128 - rank))).astype(BF16)
            w_gu = jnp.pad(w_gla_gate_up[j], ((0, 128 - rank), (0, 0))).astype(BF16)
            q, k, v, r, lg = _gla_in(x, modc[i], row(norm_pre[i, 0]), w_main, w_gz, w_gu, row(b_gla_gate[j]))
            s0_p = jnp.zeros((bp, GLA_HEADS, DVH, DKH), F32)
            s0_s = jnp.swapaxes(state_gla[j], -1, -2)
            o_p, st_p = _gla_core(q, k, v, lg, s0_p, chunk0=0, nseq=bp, nchunks=lp // CHUNK, bb=1)
            o_s, st_s = _gla_core(q, k, v, lg, s0_s, chunk0=ncp, nseq=bs, nchunks=1, bb=TILE_CHUNKS)
            o = jnp.concatenate([o_p, o_s], axis=0)
            gla_states.append((jnp.swapaxes(st_p, -1, -2), jnp.swapaxes(st_s, -1, -2)))
            x = _gla_out(o, r, x, modc[i], row(g_gla_norm[j]), w_gla_out[j].astype(BF16), row(norm_post[i, 0]))
        else:
            params = (row(norm_pre[i, 0]), w_pw1[j].astype(BF16), row(b_pw1[j]), w_dw[j], row(b_dw[j]),
                      row(ln_g[j]), row(ln_b[j]), w_pw2[j].astype(BF16), row(b_pw2[j]), row(norm_post[i, 0]))
            keep = CONV_WIDTH - 1
            h0_p = jnp.zeros((bp, HIST, d), F32)
            h0_s = jnp.pad(cache_conv[j], ((0, 0), (HIST - keep, 0), (0, 0)))
            x_p, hp = _conv(x, modc[i], h0_p, params, chunk0=0, nseq_total=bp, ntiles=lp // TILE, nseq=1)
            x_s, hs = _conv(x, modc[i], h0_s, params, chunk0=ncp, nseq_total=bs, ntiles=1, nseq=TILE_CHUNKS)
            x = jnp.concatenate([x_p, x_s], axis=0)
            conv_hists.append((hp[:, HIST - keep:], hs[:, HIST - keep:]))
        x = _moe_layer(x, modc[i], row(norm_pre[i, 1]), row(norm_post[i, 1]),
                       w_router[i], b_router[i], w_e1[i], b_e1[i], w_e2[i], b_e2[i])

    y_prompt = x[:ncp].reshape(bp, lp, d)
    y_sample = x[ncp:].reshape(bs, ls, d)
    gla_p = jnp.stack([s[0] for s in gla_states])
    gla_s = jnp.stack([s[1] for s in gla_states])
    conv_p = jnp.stack([c[0] for c in conv_hists])
    conv_s = jnp.stack([c[1] for c in conv_hists])
    return (y_prompt, y_sample, gla_p, gla_s, conv_p, conv_s)
```

```python
import functools

import jax
import jax.numpy as jnp
from jax import lax
from jax.experimental import pallas as pl
from jax.experimental.pallas import tpu as pltpu

F32 = jnp.float32
BF16 = jnp.bfloat16

D_MODEL = 1024
CHUNK = 64
TILE_CHUNKS = 4
TILE = CHUNK * TILE_CHUNKS
GLA_HEADS = 4
DKH = 128
DVH = 256
SUB = 16
GLA_SEQS = 4
GATE_TAU = 16.0
CONV_WIDTH = 31
HIST = 32
N_EXPERTS = 32
TOP_K = 4
SWIGLU_LIMIT = 7.0
SWIGLU_ALPHA = 1.702
EXPERT_ROWS = 256
EPS = 1e-6
VMEM_LIMIT = 48 * 1024 * 1024


def _cparams(sem, vmem=VMEM_LIMIT):
    return pltpu.CompilerParams(dimension_semantics=sem, vmem_limit_bytes=vmem)


def _dot(a, b):
    return jnp.dot(a, b, preferred_element_type=F32)


def _dot_nt(a, b):
    return lax.dot_general(a, b, (((1,), (1,)), ((), ())), preferred_element_type=F32)


def _dot_tn(a, b):
    return lax.dot_general(a, b, (((0,), (0,)), ((), ())), preferred_element_type=F32)


def _sigmoid(x):
    return 1.0 / (1.0 + jnp.exp(-x))


def _rms(x, g):
    return x * lax.rsqrt(jnp.mean(x * x, axis=-1, keepdims=True) + EPS) * g


def _stack_rows(rows):
    n = rows[0].shape[1]
    ri = lax.broadcasted_iota(jnp.int32, (len(rows), n), 0)
    out = jnp.broadcast_to(rows[0], (len(rows), n))
    for j in range(1, len(rows)):
        out = jnp.where(ri == j, jnp.broadcast_to(rows[j], (len(rows), n)), out)
    return out


def _split3(x):
    hi = x.astype(BF16)
    r1 = x - hi.astype(F32)
    mid = r1.astype(BF16)
    lo = (r1 - mid.astype(F32)).astype(BF16)
    return hi, mid, lo


def _ada_kernel(c_ref, w_ref, b_ref, o_ref):
    c = c_ref[...]
    cs = (c * _sigmoid(c)).astype(BF16)
    o_ref[0] = _dot(cs, w_ref[0].astype(BF16)) + b_ref[0]


def _ada(c_all, w_ada, b_ada):
    depth, d, n6 = w_ada.shape
    rows = c_all.shape[0]
    nt = n6 // d
    return pl.pallas_call(
        _ada_kernel,
        grid=(depth, nt),
        in_specs=[
            pl.BlockSpec((rows, d), lambda i, n: (0, 0)),
            pl.BlockSpec((1, d, d), lambda i, n: (i, 0, n)),
            pl.BlockSpec((1, 1, d), lambda i, n: (i, 0, n)),
        ],
        out_specs=pl.BlockSpec((1, rows, d), lambda i, n: (i, 0, n)),
        out_shape=jax.ShapeDtypeStruct((depth, rows, n6), F32),
        compiler_params=_cparams(("arbitrary", "arbitrary")),
        name="ada",
    )(c_all, w_ada, b_ada.reshape(depth, 1, n6))


def _gla_in_kernel(x_ref, mod_ref, g_ref, w_ref, wgz_ref, wgu_ref, bg_ref,
                   q_ref, k_ref, v_ref, r_ref, lg_ref):
    m = mod_ref[...]
    h = _rms(x_ref[...], g_ref[...]) * (1.0 + m[:, 1:2, :]) + m[:, 0:1, :]
    hb = h.reshape(TILE, D_MODEL).astype(BF16)
    p = _dot(hb, w_ref[...])
    dk = GLA_HEADS * DKH
    dv = GLA_HEADS * DVH
    q_ref[...] = (p[:, :dk] * (DKH ** -0.5)).reshape(TILE_CHUNKS, CHUNK, dk)
    k_ref[...] = p[:, dk:2 * dk].reshape(TILE_CHUNKS, CHUNK, dk)
    v_ref[...] = p[:, 2 * dk:2 * dk + dv].reshape(TILE_CHUNKS, CHUNK, dv)
    r_ref[...] = p[:, 2 * dk + dv:].reshape(TILE_CHUNKS, CHUNK, dv)
    gz = _dot(hb, wgz_ref[...])
    z = _dot(gz.astype(BF16), wgu_ref[...]) + bg_ref[...]
    ls = jnp.minimum(z, 0.0) - jnp.log(1.0 + jnp.exp(-jnp.abs(z)))
    lg_ref[...] = (ls / GATE_TAU).reshape(TILE_CHUNKS, CHUNK, dk)


def _gla_in(x, modc, g_pre, w_main, w_gz, w_gu, b_gate):
    nc = x.shape[0]
    dk = GLA_HEADS * DKH
    dv = GLA_HEADS * DVH
    tok = lambda w: pl.BlockSpec((TILE_CHUNKS, CHUNK, w), lambda i: (i, 0, 0))
    full = lambda a: pl.BlockSpec(a.shape, lambda i: (0,) * a.ndim)
    outs = [jax.ShapeDtypeStruct((nc, CHUNK, w), F32) for w in (dk, dk, dv, dv, dk)]
    return pl.pallas_call(
        _gla_in_kernel,
        grid=(nc // TILE_CHUNKS,),
        in_specs=[tok(D_MODEL), pl.BlockSpec((TILE_CHUNKS, 6, D_MODEL), lambda i: (i, 0, 0)),
                  full(g_pre), full(w_main), full(w_gz), full(w_gu), full(b_gate)],
        out_specs=[tok(dk), tok(dk), tok(dv), tok(dv), tok(dk)],
        out_shape=outs,
        compiler_params=_cparams(("arbitrary",)),
        name="gla_in",
    )(x, modc, g_pre, w_main, w_gz, w_gu, b_gate)


def _gla_core_kernel(*refs, nchunks):
    ns = GLA_SEQS
    q_refs, k_refs, v_refs, lg_refs = (refs[g * ns:(g + 1) * ns] for g in range(4))
    s0_ref = refs[4 * ns]
    o_refs = refs[4 * ns + 1:5 * ns + 1]
    sout_ref, s_scr = refs[5 * ns + 1], refs[5 * ns + 2]
    j = pl.program_id(1)

    @pl.when(j == 0)
    def _():
        s_scr[...] = s0_ref[...]

    nsub = CHUNK // SUB
    npair = SUB * SUB
    rr = lax.broadcasted_iota(jnp.int32, (CHUNK, CHUNK), 0)
    cc = lax.broadcasted_iota(jnp.int32, (CHUNK, CHUNK), 1)
    tril = (cc <= rr).astype(BF16)
    ones_k = jnp.ones((DKH, DKH), BF16)
    pr = lax.broadcasted_iota(jnp.int32, (nsub * npair, 2 * CHUNK), 0)
    pc = lax.broadcasted_iota(jnp.int32, (nsub * npair, 2 * CHUNK), 1)
    ps, pt, pi = pr % SUB, (pr // SUB) % SUB, pr // npair
    place = ((pc == SUB * pi + ps) & (ps <= pt)).astype(F32)
    st = lax.broadcasted_iota(jnp.int32, (CHUNK, nsub * npair), 0)
    sp = lax.broadcasted_iota(jnp.int32, (CHUNK, nsub * npair), 1)
    sel = (sp // SUB == st).astype(BF16)
    row_sub = lax.broadcasted_iota(jnp.int32, (CHUNK, DKH), 0) // SUB
    key_row = lax.broadcasted_iota(jnp.int32, (CHUNK, DKH), 0)
    zeros_v = jnp.zeros((CHUNK, DVH), BF16)

    for b in range(ns):
        lg = lg_refs[b][0]
        hi, mid, lo = _split3(lg)
        bcum = _dot(tril, hi) + _dot(tril, mid) + _dot(tril, lo)
        q_all = q_refs[b][0]
        k_all = k_refs[b][0]
        v_all = v_refs[b][0]
        heads = range(GLA_HEADS)
        qs = [q_all[:, h * DKH:(h + 1) * DKH] for h in heads]
        ks = [k_all[:, h * DKH:(h + 1) * DKH] for h in heads]
        bs_ = [bcum[:, h * DKH:(h + 1) * DKH] for h in heads]
        vbs = [v_all[:, h * DVH:(h + 1) * DVH].astype(BF16) for h in heads]
        outs = []
        for h in heads:
            blast = bs_[h][CHUNK - 1:CHUNK, :]
            s_t = s_scr[b, h]
            outs.append(_dot_nt((qs[h] * jnp.exp(bs_[h])).astype(BF16), s_t.astype(BF16)))
            kdec = (ks[h] * jnp.exp(blast - bs_[h])).astype(BF16)
            s_scr[b, h] = s_t * jnp.exp(blast) + _dot_tn(vbs[h], kdec)
        pairs = []
        for h in heads:
            qh, kh, bh = qs[h], ks[h], bs_[h]
            qe = jnp.concatenate([jnp.broadcast_to(qh[t:t + 1], (SUB, DKH)) for t in range(CHUNK)], axis=0)
            be = jnp.concatenate([jnp.broadcast_to(bh[t:t + 1], (SUB, DKH)) for t in range(CHUNK)], axis=0)
            kt = jnp.concatenate([kh[SUB * i:SUB * (i + 1)] for i in range(nsub) for _ in range(SUB)], axis=0)
            bt = jnp.concatenate([bh[SUB * i:SUB * (i + 1)] for i in range(nsub) for _ in range(SUB)], axis=0)
            pairs.append((qe * kt * jnp.exp(jnp.minimum(be - bt, 0.0))).astype(BF16))
        rsums = [_dot(pairs[h], ones_k) for h in heads]
        a_diag = [_dot(sel, (rsums[h] * place).astype(BF16)) for h in heads]
        a_off = []
        for h in heads:
            qh, kh, bh = qs[h], ks[h], bs_[h]
            qd, kd = [], []
            for i in range(1, nsub):
                anc = bh[SUB * i - 1:SUB * i, :]
                qd.append(jnp.where(row_sub == i, qh * jnp.exp(jnp.minimum(bh - anc, 0.0)), 0.0))
                kd.append(jnp.where(key_row < SUB * i, kh * jnp.exp(jnp.minimum(anc - bh, 0.0)), 0.0))
            qd = jnp.concatenate(qd, axis=-1).astype(BF16)
            kd = jnp.concatenate(kd, axis=-1).astype(BF16)
            a_off.append(_dot_nt(qd, jnp.concatenate([kd, jnp.zeros_like(kd)], axis=0)))
        for h in heads:
            a = (a_diag[h] + a_off[h]).astype(BF16)
            o = outs[h] + _dot(a, jnp.concatenate([vbs[h], zeros_v], axis=0))
            o_refs[b][0, :, h * DVH:(h + 1) * DVH] = o

    @pl.when(j == nchunks - 1)
    def _():
        sout_ref[...] = s_scr[...]


def _gla_core(q, k, v, lg, s0_t, *, chunk0, nblocks, nchunks):
    dk = GLA_HEADS * DKH
    dv = GLA_HEADS * DVH
    ns = GLA_SEQS

    def tok(w, g):
        return pl.BlockSpec((1, CHUNK, w), lambda s, j: (chunk0 + (s * ns + g) * nchunks + j, 0, 0))

    st_spec = pl.BlockSpec((ns, GLA_HEADS, DVH, DKH), lambda s, j: (s, 0, 0, 0))
    o_spec = pl.BlockSpec((1, CHUNK, dv), lambda s, j: (s * nchunks + j, 0, 0))
    in_specs = [tok(w, g) for w in (dk, dk, dv, dk) for g in range(ns)] + [st_spec]
    args = [a for a in (q, k, v, lg) for _ in range(ns)] + [s0_t]
    res = pl.pallas_call(
        functools.partial(_gla_core_kernel, nchunks=nchunks),
        grid=(nblocks, nchunks),
        in_specs=in_specs,
        out_specs=[o_spec] * ns + [st_spec],
        out_shape=[jax.ShapeDtypeStruct((nblocks * nchunks, CHUNK, dv), F32)] * ns
        + [jax.ShapeDtypeStruct((nblocks * ns, GLA_HEADS, DVH, DKH), F32)],
        scratch_shapes=[pltpu.VMEM((ns, GLA_HEADS, DVH, DKH), F32)],
        compiler_params=_cparams(("arbitrary", "arbitrary")),
        name="gla_core",
    )(*args)
    return res[:ns], res[ns]


def _gla_out_kernel(o_ref, r_ref, x_ref, mod_ref, gn_ref, w_ref, gpost_ref, x1_ref):
    o = o_ref[...].reshape(TILE, GLA_HEADS * DVH)
    r = r_ref[...].reshape(TILE, GLA_HEADS * DVH)
    gn = gn_ref[...]
    parts = [_rms(o[:, h * DVH:(h + 1) * DVH], gn) for h in range(GLA_HEADS)]
    y = jnp.concatenate(parts, axis=-1) * (r * _sigmoid(r))
    mix = _dot(y.astype(BF16), w_ref[...]).reshape(TILE_CHUNKS, CHUNK, D_MODEL)
    m = mod_ref[...]
    x1_ref[...] = x_ref[...] + m[:, 2:3, :] * _rms(mix, gpost_ref[...])


def _gla_out(o, r, x, modc, g_norm, w_out, g_post):
    nc = x.shape[0]
    tok = lambda w: pl.BlockSpec((TILE_CHUNKS, CHUNK, w), lambda i: (i, 0, 0))
    full = lambda a: pl.BlockSpec(a.shape, lambda i: (0,) * a.ndim)
    return pl.pallas_call(
        _gla_out_kernel,
        grid=(nc // TILE_CHUNKS,),
        in_specs=[tok(GLA_HEADS * DVH), tok(GLA_HEADS * DVH), tok(D_MODEL),
                  pl.BlockSpec((TILE_CHUNKS, 6, D_MODEL), lambda i: (i, 0, 0)),
                  full(g_norm), full(w_out), full(g_post)],
        out_specs=tok(D_MODEL),
        out_shape=jax.ShapeDtypeStruct(x.shape, F32),
        compiler_params=_cparams(("arbitrary",)),
        name="gla_out",
    )(o, r, x, modc, g_norm, w_out, g_post)


def _conv_kernel(x_ref, mod_ref, hist0_ref, gpre_ref, w1_ref, b1_ref, wdw_ref, bdw_ref, lng_ref, lnb_ref,
                 w2_ref, b2_ref, gpost_ref, x1_ref, hist_out_ref, ext_scr, *, nseq, ntiles):
    j = pl.program_id(1)
    tt = TILE // nseq

    @pl.when(j == 0)
    def _():
        for s in range(nseq):
            ext_scr[s, 0:HIST, :] = hist0_ref[s]

    m = mod_ref[...]
    h = _rms(x_ref[...], gpre_ref[...]) * (1.0 + m[:, 1:2, :]) + m[:, 0:1, :]
    p = _dot(h.reshape(TILE, D_MODEL).astype(BF16), w1_ref[...]) + b1_ref[...]
    u = p[:, :D_MODEL] * _sigmoid(p[:, D_MODEL:])
    first = HIST - (CONV_WIDTH - 1)
    ys = []
    for s in range(nseq):
        ext_scr[s, HIST:HIST + tt, :] = u[s * tt:(s + 1) * tt]
        cols = []
        for c in range(D_MODEL // 128):
            lanes = slice(c * 128, (c + 1) * 128)
            acc = jnp.zeros((tt, 128), F32)
            for tap in range(CONV_WIDTH):
                acc = acc + ext_scr[s, first + tap:first + tap + tt, lanes] * wdw_ref[tap:tap + 1, lanes]
            cols.append(acc)
        ys.append(jnp.concatenate(cols, axis=-1))
        hist_new = ext_scr[s, tt:tt + HIST, :]
        ext_scr[s, 0:HIST, :] = hist_new

        @pl.when(j == ntiles - 1)
        def _():
            hist_out_ref[s] = hist_new
    y = jnp.concatenate(ys, axis=0) + bdw_ref[...]
    yc = y - jnp.mean(y, axis=-1, keepdims=True)
    yn = yc * lax.rsqrt(jnp.mean(yc * yc, axis=-1, keepdims=True) + EPS) * lng_ref[...] + lnb_ref[...]
    act = yn * _sigmoid(yn)
    mix = (_dot(act.astype(BF16), w2_ref[...]) + b2_ref[...]).reshape(TILE_CHUNKS, CHUNK, D_MODEL)
    x1_ref[...] = x_ref[...] + m[:, 2:3, :] * _rms(mix, gpost_ref[...])


def _conv(x, modc, hist0, params, *, chunk0, nseq_total, ntiles, nseq):
    tile0 = chunk0 // TILE_CHUNKS
    if ntiles == 1:
        cidx = lambda s, j: (tile0 + s, 0, 0)
        oidx = lambda s, j: (s, 0, 0)
    else:
        assert nseq == 1
        cidx = lambda s, j: (tile0 + s * ntiles + j, 0, 0)
        oidx = lambda s, j: (s * ntiles + j, 0, 0)
    full = lambda a: pl.BlockSpec(a.shape, lambda s, j: (0,) * a.ndim)
    hist_spec = pl.BlockSpec((nseq, HIST, D_MODEL), lambda s, j: (s, 0, 0))
    nblocks = nseq_total // nseq
    return pl.pallas_call(
        functools.partial(_conv_kernel, nseq=nseq, ntiles=ntiles),
        grid=(nblocks, ntiles),
        in_specs=[pl.BlockSpec((TILE_CHUNKS, CHUNK, D_MODEL), cidx),
                  pl.BlockSpec((TILE_CHUNKS, 6, D_MODEL), cidx),
                  hist_spec] + [full(a) for a in params],
        out_specs=[pl.BlockSpec((TILE_CHUNKS, CHUNK, D_MODEL), oidx), hist_spec],
        out_shape=[jax.ShapeDtypeStruct((nblocks * ntiles * TILE_CHUNKS, CHUNK, D_MODEL), F32),
                   jax.ShapeDtypeStruct((nseq_total, HIST, D_MODEL), F32)],
        scratch_shapes=[pltpu.VMEM((nseq, HIST + TILE // nseq, D_MODEL), F32)],
        compiler_params=_cparams(("arbitrary", "arbitrary")),
        name="conv",
    )(x, modc, hist0, *params)


def _router_kernel(x_ref, mod_ref, g_ref, wrt_ref, br_ref,
                   h_ref, idx_ref, gate_ref, rank_ref, cnt_ref, carry_scr):
    i = pl.program_id(0)

    @pl.when(i == 0)
    def _():
        carry_scr[...] = jnp.zeros_like(carry_scr)

    m = mod_ref[...]
    h = _rms(x_ref[...], g_ref[...]) * (1.0 + m[:, 4:5, :]) + m[:, 3:4, :]
    h_ref[...] = h
    h2 = h.reshape(TILE, D_MODEL)
    h_hi = h2.astype(BF16)
    h_lo = (h2 - h_hi.astype(F32)).astype(BF16)
    w = wrt_ref[...]
    w_hi = w.astype(BF16)
    w_lo = (w - w_hi.astype(F32)).astype(BF16)
    logits = _dot_nt(w_hi, h_hi) + _dot_nt(w_lo, h_hi) + _dot_nt(w_hi, h_lo) + br_ref[...]
    e_iota = lax.broadcasted_iota(jnp.int32, (N_EXPERTS, TILE), 0)
    vals, idxs, sels = [], [], []
    lt = logits
    for _ in range(TOP_K):
        mx = jnp.max(lt, axis=0, keepdims=True)
        ix = jnp.min(jnp.where(lt == mx, e_iota, N_EXPERTS), axis=0, keepdims=True)
        sl = e_iota == ix
        vals.append(mx)
        idxs.append(ix)
        sels.append(sl)
        lt = jnp.where(sl, -jnp.inf, lt)
    ex = [jnp.exp(v - vals[0]) for v in vals]
    den = ex[0] + ex[1] + ex[2] + ex[3]
    gate_ref[...] = _stack_rows([e / den for e in ex])
    idx_ref[...] = _stack_rows(idxs)
    chosen = (sels[0] | sels[1] | sels[2] | sels[3])
    chosen_f = jnp.where(chosen, 1.0, 0.0)
    rr = lax.broadcasted_iota(jnp.int32, (TILE, TILE), 0)
    cc = lax.broadcasted_iota(jnp.int32, (TILE, TILE), 1)
    before = (rr < cc).astype(BF16)
    pos = carry_scr[...][:, 0:1] + _dot(chosen_f.astype(BF16), before)
    ranks = [jnp.sum(jnp.where(sl, pos, 0.0), axis=0, keepdims=True) for sl in sels]
    rank_ref[...] = _stack_rows(ranks).astype(jnp.int32)
    carry_scr[...] = carry_scr[...] + jnp.sum(chosen_f, axis=1, keepdims=True)
    cnt_ref[...] = carry_scr[...]


def _router(x, modc, g_pre, w_router_t, b_router):
    nc = x.shape[0]
    nt = nc // TILE_CHUNKS
    n = nc * CHUNK
    full = lambda a: pl.BlockSpec(a.shape, lambda i: (0,) * a.ndim)
    tok = pl.BlockSpec((TILE_CHUNKS, CHUNK, D_MODEL), lambda i: (i, 0, 0))
    kt = pl.BlockSpec((TOP_K, TILE), lambda i: (0, i))
    return pl.pallas_call(
        _router_kernel,
        grid=(nt,),
        in_specs=[tok, pl.BlockSpec((TILE_CHUNKS, 6, D_MODEL), lambda i: (i, 0, 0)),
                  full(g_pre), full(w_router_t), full(b_router)],
        out_specs=[tok, kt, kt, kt, pl.BlockSpec((N_EXPERTS, 128), lambda i: (0, 0))],
        out_shape=[jax.ShapeDtypeStruct(x.shape, F32),
                   jax.ShapeDtypeStruct((TOP_K, n), jnp.int32),
                   jax.ShapeDtypeStruct((TOP_K, n), F32),
                   jax.ShapeDtypeStruct((TOP_K, n), jnp.int32),
                   jax.ShapeDtypeStruct((N_EXPERTS, 128), F32)],
        scratch_shapes=[pltpu.VMEM((N_EXPERTS, 128), F32)],
        compiler_params=_cparams(("arbitrary",)),
        name="router",
    )(x, modc, g_pre, w_router_t, b_router)


def _row_copy(src, dst, sem):
    return pltpu.make_async_copy(src, dst, sem)


def _dispatch_kernel(dest_ref, pend_ref, h_ref, rows_ref, zero_scr, sem, zsem):
    i = pl.program_id(0)

    @pl.when(i == 0)
    def _():
        zero_scr[...] = jnp.zeros_like(zero_scr)

        n_blocks = rows_ref.shape[0] // EXPERT_ROWS
        total = pend_ref[N_EXPERTS - 1]

        def block_to_clear(e):
            is_tail = e >= N_EXPERTS
            ee = jnp.minimum(e, N_EXPERTS - 1)
            end = pend_ref[ee]
            start = jnp.where(ee == 0, 0, pend_ref[jnp.maximum(ee - 1, 0)])
            tail_row = total + (e - N_EXPERTS) * EXPERT_ROWS
            row0 = jnp.where(is_tail, tail_row, jnp.maximum(end - EXPERT_ROWS, 0))
            needed = jnp.where(is_tail, tail_row < n_blocks * EXPERT_ROWS, end > start)
            row0 = jnp.minimum(row0, (n_blocks - 1) * EXPERT_ROWS)
            dst = rows_ref.at[pl.ds(pl.multiple_of(row0, EXPERT_ROWS), EXPERT_ROWS)]
            return needed, _row_copy(zero_scr, dst, zsem)

        def clear(e, c):
            needed, cp = block_to_clear(e)

            @pl.when(needed)
            def _():
                cp.start()
            return c

        def settle(e, c):
            needed, cp = block_to_clear(e)

            @pl.when(needed)
            def _():
                cp.wait()
            return c

        lax.fori_loop(0, 2 * N_EXPERTS, clear, 0)
        lax.fori_loop(0, 2 * N_EXPERTS, settle, 0)

    base = i * (TOP_K * TILE)

    def issue(r, c):
        for k in range(TOP_K):
            d = dest_ref[base + k * TILE + r]
            _row_copy(h_ref.at[pl.ds(r, 1)], rows_ref.at[pl.ds(d, 1)], sem).start()
        return c

    lax.fori_loop(0, TILE, issue, 0, unroll=8)

    def drain(r, c):
        for k in range(TOP_K):
            _row_copy(h_ref.at[pl.ds(0, 1)], rows_ref.at[pl.ds(0, 1)], sem).wait()
        return c

    lax.fori_loop(0, TILE, drain, 0, unroll=8)


def _dispatch(dest_flat, p_ends, h, n_rows):
    n = h.shape[0]
    return pl.pallas_call(
        _dispatch_kernel,
        grid_spec=pltpu.PrefetchScalarGridSpec(
            num_scalar_prefetch=2,
            grid=(n // TILE,),
            in_specs=[pl.BlockSpec((TILE, D_MODEL), lambda i, dest, pend: (i, 0))],
            out_specs=pl.BlockSpec(memory_space=pl.ANY),
            scratch_shapes=[pltpu.VMEM((EXPERT_ROWS, D_MODEL), F32),
                            pltpu.SemaphoreType.DMA, pltpu.SemaphoreType.DMA],
        ),
        out_shape=jax.ShapeDtypeStruct((n_rows, D_MODEL), F32),
        compiler_params=_cparams(("arbitrary",)),
        name="dispatch",
    )(dest_flat, p_ends, h)


def _experts_kernel(be_ref, nu_ref, x_ref, w1_ref, b1_ref, w2_ref, b2_ref, y_ref, w1_scr, w2_scr):
    i = pl.program_id(0)
    valid = i < nu_ref[0]
    prev = be_ref[jnp.maximum(i - 1, 0)]
    changed = jnp.logical_or(i == 0, be_ref[i] != prev)
    dff = w2_scr.shape[0]

    @pl.when(jnp.logical_and(valid, changed))
    def _():
        rows = 128

        def cast(c, carry):
            r0 = pl.multiple_of(c * rows, rows)
            w1_scr[pl.ds(r0, rows), :] = w1_ref[0, 0, pl.ds(r0, rows), :].astype(BF16)
            w2_scr[pl.ds(r0, rows), :] = w2_ref[0, 0, pl.ds(r0, rows), :].astype(BF16)
            return carry

        lax.fori_loop(0, D_MODEL // rows, cast, 0)

    @pl.when(valid)
    def _():
        h = _dot(x_ref[...].astype(BF16), w1_scr[...]) + b1_ref[0, 0]
        gate = jnp.minimum(h[:, :dff], SWIGLU_LIMIT)
        up = jnp.clip(h[:, dff:], -SWIGLU_LIMIT, SWIGLU_LIMIT)
        act = (up + 1.0) * (gate * _sigmoid(SWIGLU_ALPHA * gate))
        y_ref[...] = _dot(act.astype(BF16), w2_scr[...]) + b2_ref[0, 0]

    @pl.when(jnp.logical_not(valid))
    def _():
        y_ref[...] = jnp.zeros_like(y_ref)


def _experts(blk_e, n_used, x_rows, w1, b1, w2, b2, layer):
    n_rows = x_rows.shape[0]
    n_blocks = n_rows // EXPERT_ROWS
    nl, e, d, f2 = w1.shape
    dff = w2.shape[2]
    assert d == dff == D_MODEL
    row_idx = lambda i, be, nu: (jnp.minimum(i, nu[0] - 1), 0)
    w_idx = lambda i, be, nu: (layer, be[i], 0, 0)
    return pl.pallas_call(
        _experts_kernel,
        grid_spec=pltpu.PrefetchScalarGridSpec(
            num_scalar_prefetch=2,
            grid=(n_blocks,),
            in_specs=[pl.BlockSpec((EXPERT_ROWS, D_MODEL), row_idx),
                      pl.BlockSpec((1, 1, d, f2), w_idx),
                      pl.BlockSpec((1, 1, 1, f2), w_idx),
                      pl.BlockSpec((1, 1, dff, D_MODEL), w_idx),
                      pl.BlockSpec((1, 1, 1, D_MODEL), w_idx)],
            out_specs=pl.BlockSpec((EXPERT_ROWS, D_MODEL), lambda i, be, nu: (i, 0)),
            scratch_shapes=[pltpu.VMEM((d, f2), BF16), pltpu.VMEM((dff, D_MODEL), BF16)],
        ),
        out_shape=jax.ShapeDtypeStruct((n_rows, D_MODEL), F32),
        compiler_params=_cparams(("arbitrary",)),
        name="experts",
    )(blk_e, n_used, x_rows, w1, b1.reshape(nl, e, 1, f2), w2, b2.reshape(nl, e, 1, D_MODEL))


def _combine_kernel(dest_ref, gate_ref, x_ref, mod_ref, gpost_ref, rows_ref, o_ref, ybuf, sem):
    base = pl.program_id(0) * (TOP_K * TILE)

    def issue(r, c):
        for k in range(TOP_K):
            d = dest_ref[base + k * TILE + r]
            _row_copy(rows_ref.at[pl.ds(d, 1)], ybuf.at[k, pl.ds(r, 1)], sem).start()
        return c

    lax.fori_loop(0, TILE, issue, 0, unroll=8)

    def drain(r, c):
        for k in range(TOP_K):
            _row_copy(rows_ref.at[pl.ds(0, 1)], ybuf.at[0, pl.ds(0, 1)], sem).wait()
        return c

    lax.fori_loop(0, TILE, drain, 0, unroll=8)

    g = gate_ref[...]
    y = g[:, 0:1] * ybuf[0]
    for k in range(1, TOP_K):
        y = y + g[:, k:k + 1] * ybuf[k]
    m = mod_ref[...]
    y = y.reshape(TILE_CHUNKS, CHUNK, D_MODEL)
    o_ref[...] = x_ref[...] + m[:, 5:6, :] * _rms(y, gpost_ref[...])


def _combine(dest_flat, gates, x, modc, g_post, y_rows):
    nc = x.shape[0]
    tok = pl.BlockSpec((TILE_CHUNKS, CHUNK, D_MODEL), lambda i, dest: (i, 0, 0))
    return pl.pallas_call(
        _combine_kernel,
        grid_spec=pltpu.PrefetchScalarGridSpec(
            num_scalar_prefetch=1,
            grid=(nc // TILE_CHUNKS,),
            in_specs=[pl.BlockSpec((TILE, TOP_K), lambda i, dest: (i, 0)),
                      tok,
                      pl.BlockSpec((TILE_CHUNKS, 6, D_MODEL), lambda i, dest: (i, 0, 0)),
                      pl.BlockSpec(g_post.shape, lambda i, dest: (0, 0)),
                      pl.BlockSpec(memory_space=pl.ANY)],
            out_specs=tok,
            scratch_shapes=[pltpu.VMEM((TOP_K, TILE, D_MODEL), F32), pltpu.SemaphoreType.DMA],
        ),
        out_shape=jax.ShapeDtypeStruct(x.shape, F32),
        compiler_params=_cparams(("arbitrary",)),
        name="combine",
    )(dest_flat, gates, x, modc, g_post, y_rows)


def _moe_layer(x, modc, g_pre, g_post, w_router, b_router, w1, b1, w2, b2, layer):
    nc = x.shape[0]
    n = nc * CHUNK
    nt = n // TILE
    h, idx_t, gate_t, rank_t, cnt = _router(x, modc, g_pre, w_router.T, b_router.reshape(N_EXPERTS, 1))
    counts = cnt[:, 0].astype(jnp.int32)
    padded = (counts + EXPERT_ROWS - 1) // EXPERT_ROWS * EXPERT_ROWS
    p_ends = jnp.cumsum(padded).astype(jnp.int32)
    p_starts = p_ends - padded
    n_blocks = -(-(n * TOP_K) // EXPERT_ROWS) + N_EXPERTS
    n_used = p_ends[-1] // EXPERT_ROWS
    blk = jnp.minimum(jnp.arange(n_blocks, dtype=jnp.int32), n_used - 1) * EXPERT_ROWS
    blk_e = jnp.minimum(jnp.sum((p_ends[None, :] <= blk[:, None]).astype(jnp.int32), axis=1), N_EXPERTS - 1)
    eids = jnp.arange(N_EXPERTS, dtype=jnp.int32)
    start_of = jnp.sum(jnp.where(idx_t[:, :, None] == eids, p_starts, 0), axis=-1)
    dest_t = start_of + rank_t
    dest_flat = dest_t.reshape(TOP_K, nt, TILE).transpose(1, 0, 2).reshape(-1)
    x_rows = _dispatch(dest_flat, p_ends, h.reshape(n, D_MODEL), n_blocks * EXPERT_ROWS)
    y_rows = _experts(blk_e, n_used.reshape(1), x_rows, w1, b1, w2, b2, layer)
    return _combine(dest_flat, gate_t.T, x, modc, g_post, y_rows)


def kernel(x_prompt, x_sample, state_gla, cache_conv, c_prompt, c_sample, w_ada, b_ada, norm_pre, norm_post, w_gla_in, w_gla_gate_up, b_gla_gate, g_gla_norm, w_gla_out, w_pw1, b_pw1, w_dw, b_dw, ln_g, ln_b, w_pw2, b_pw2, w_router, b_router, w_e1, b_e1, w_e2, b_e2):
    bp, lp, d = x_prompt.shape
    bs, ls, _ = x_sample.shape
    depth = w_ada.shape[0]
    assert d == D_MODEL and ls == CHUNK and lp % TILE == 0 and bs % TILE_CHUNKS == 0
    assert bp == GLA_SEQS and bs % GLA_SEQS == 0
    ncp = bp * lp // CHUNK
    ncs = bs
    nseq = bp + bs
    nch = lp // CHUNK

    x = jnp.concatenate([x_prompt.reshape(ncp, CHUNK, d), x_sample.reshape(ncs, CHUNK, d)], axis=0)
    rows = -(-nseq // 8) * 8
    c_all = jnp.concatenate([c_prompt, c_sample, jnp.zeros((rows - nseq, d), F32)], axis=0)
    mod = _ada(c_all, w_ada, b_ada)
    mod_p = jnp.broadcast_to(mod[:, :bp, None, :], (depth, bp, nch, 6 * d)).reshape(depth, ncp, 6 * d)
    modc = jnp.concatenate([mod_p, mod[:, bp:nseq]], axis=1).reshape(depth, ncp + ncs, 6, d)

    row = lambda a: a.reshape(1, -1)
    dk = GLA_HEADS * DKH
    dv = GLA_HEADS * DVH
    gla_states, conv_hists = [], []
    for i in range(depth):
        j = i // 2
        if i % 2 == 0:
            w_in = w_gla_in[j]
            rank = w_gla_gate_up.shape[1]
            w_main = w_in[:, :2 * dk + 2 * dv].astype(BF16)
            w_gz = jnp.pad(w_in[:, 2 * dk + 2 * dv:], ((0, 0), (0, 128 - rank))).astype(BF16)
            w_gu = jnp.pad(w_gla_gate_up[j], ((0, 128 - rank), (0, 0))).astype(BF16)
            q, k, v, r, lg = _gla_in(x, modc[i], row(norm_pre[i, 0]), w_main, w_gz, w_gu, row(b_gla_gate[j]))
            s0_p = jnp.zeros((bp, GLA_HEADS, DVH, DKH), F32)
            s0_s = jnp.swapaxes(state_gla[j], -1, -2)
            o_p, st_p = _gla_core(q, k, v, lg, s0_p, chunk0=0, nblocks=1, nchunks=nch)
            o_s, st_s = _gla_core(q, k, v, lg, s0_s, chunk0=ncp, nblocks=bs // GLA_SEQS, nchunks=1)
            o_s = jnp.stack(o_s, axis=1).reshape(ncs, CHUNK, dv)
            o = jnp.concatenate(list(o_p) + [o_s], axis=0)
            gla_states.append((jnp.swapaxes(st_p, -1, -2), jnp.swapaxes(st_s, -1, -2)))
            x = _gla_out(o, r, x, modc[i], row(g_gla_norm[j]), w_gla_out[j].astype(BF16), row(norm_post[i, 0]))
        else:
            params = (row(norm_pre[i, 0]), w_pw1[j].astype(BF16), row(b_pw1[j]), w_dw[j], row(b_dw[j]),
                      row(ln_g[j]), row(ln_b[j]), w_pw2[j].astype(BF16), row(b_pw2[j]), row(norm_post[i, 0]))
            keep = CONV_WIDTH - 1
            h0_p = jnp.zeros((bp, HIST, d), F32)
            h0_s = jnp.pad(cache_conv[j], ((0, 0), (HIST - keep, 0), (0, 0)))
            x_p, hp = _conv(x, modc[i], h0_p, params, chunk0=0, nseq_total=bp, ntiles=lp // TILE, nseq=1)
            x_s, hs = _conv(x, modc[i], h0_s, params, chunk0=ncp, nseq_total=bs, ntiles=1, nseq=TILE_CHUNKS)
            x = jnp.concatenate([x_p, x_s], axis=0)
            conv_hists.append((hp[:, HIST - keep:], hs[:, HIST - keep:]))
        x = _moe_layer(x, modc[i], row(norm_pre[i, 1]), row(norm_post[i, 1]),
                       w_router[i], b_router[i], w_e1, b_e1, w_e2, b_e2, i)

    y_prompt = x[:ncp].reshape(bp, lp, d)
    y_sample = x[ncp:].reshape(bs, ls, d)
    gla_p = jnp.stack([s[0] for s in gla_states])
    gla_s = jnp.stack([s[1] for s in gla_states])
    conv_p = jnp.stack([c[0] for c in conv_hists])
    conv_s = jnp.stack([c[1] for c in conv_hists])
    return (y_prompt, y_sample, gla_p, gla_s, conv_p, conv_s)
```

```python
import functools

import jax
import jax.numpy as jnp
from jax import lax
from jax.experimental import pallas as pl
from jax.experimental.pallas import tpu as pltpu

F32 = jnp.float32
BF16 = jnp.bfloat16

D_MODEL = 1024
CHUNK = 64
TILE_CHUNKS = 4
TILE = CHUNK * TILE_CHUNKS
GLA_HEADS = 4
DKH = 128
DVH = 256
SUB = 16
GLA_SEQS = 4
GATE_TAU = 16.0
CONV_WIDTH = 31
HIST = 32
N_EXPERTS = 32
TOP_K = 4
SWIGLU_LIMIT = 7.0
SWIGLU_ALPHA = 1.702
EXPERT_ROWS = 256
EPS = 1e-6
VMEM_LIMIT = 48 * 1024 * 1024
EXPERTS_VMEM_LIMIT = 56 * 1024 * 1024


def _cparams(sem, vmem=VMEM_LIMIT):
    return pltpu.CompilerParams(dimension_semantics=sem, vmem_limit_bytes=vmem)


def _dot(a, b):
    return jnp.dot(a, b, preferred_element_type=F32)


def _dot_nt(a, b):
    return lax.dot_general(a, b, (((1,), (1,)), ((), ())), preferred_element_type=F32)


def _dot_tn(a, b):
    return lax.dot_general(a, b, (((0,), (0,)), ((), ())), preferred_element_type=F32)


def _sigmoid(x):
    return 1.0 / (1.0 + jnp.exp(-x))


def _rms(x, g):
    return x * lax.rsqrt(jnp.mean(x * x, axis=-1, keepdims=True) + EPS) * g


def _stack_rows(rows):
    n = rows[0].shape[1]
    ri = lax.broadcasted_iota(jnp.int32, (len(rows), n), 0)
    out = jnp.broadcast_to(rows[0], (len(rows), n))
    for j in range(1, len(rows)):
        out = jnp.where(ri == j, jnp.broadcast_to(rows[j], (len(rows), n)), out)
    return out


def _split3(x):
    hi = x.astype(BF16)
    r1 = x - hi.astype(F32)
    mid = r1.astype(BF16)
    lo = (r1 - mid.astype(F32)).astype(BF16)
    return hi, mid, lo


def _ada_kernel(c_ref, w_ref, b_ref, o_ref):
    c = c_ref[...]
    cs = (c * _sigmoid(c)).astype(BF16)
    o_ref[0] = _dot(cs, w_ref[0].astype(BF16)) + b_ref[0]


def _ada(c_all, w_ada, b_ada):
    depth, d, n6 = w_ada.shape
    rows = c_all.shape[0]
    nt = n6 // d
    return pl.pallas_call(
        _ada_kernel,
        grid=(depth, nt),
        in_specs=[
            pl.BlockSpec((rows, d), lambda i, n: (0, 0)),
            pl.BlockSpec((1, d, d), lambda i, n: (i, 0, n)),
            pl.BlockSpec((1, 1, d), lambda i, n: (i, 0, n)),
        ],
        out_specs=pl.BlockSpec((1, rows, d), lambda i, n: (i, 0, n)),
        out_shape=jax.ShapeDtypeStruct((depth, rows, n6), F32),
        compiler_params=_cparams(("arbitrary", "arbitrary")),
        name="ada",
    )(c_all, w_ada, b_ada.reshape(depth, 1, n6))


def _gla_in_kernel(x_ref, mod_ref, g_ref, w_ref, wgz_ref, wgu_ref, bg_ref,
                   q_ref, k_ref, v_ref, r_ref, lg_ref):
    m = mod_ref[...]
    h = _rms(x_ref[...], g_ref[...]) * (1.0 + m[:, 1:2, :]) + m[:, 0:1, :]
    hb = h.reshape(TILE, D_MODEL).astype(BF16)
    p = _dot(hb, w_ref[...])
    dk = GLA_HEADS * DKH
    dv = GLA_HEADS * DVH
    q_ref[...] = (p[:, :dk] * (DKH ** -0.5)).reshape(TILE_CHUNKS, CHUNK, dk)
    k_ref[...] = p[:, dk:2 * dk].reshape(TILE_CHUNKS, CHUNK, dk)
    v_ref[...] = p[:, 2 * dk:2 * dk + dv].reshape(TILE_CHUNKS, CHUNK, dv)
    r_ref[...] = p[:, 2 * dk + dv:].reshape(TILE_CHUNKS, CHUNK, dv)
    gz = _dot(hb, wgz_ref[...])
    z = _dot(gz.astype(BF16), wgu_ref[...]) + bg_ref[...]
    ls = jnp.minimum(z, 0.0) - jnp.log(1.0 + jnp.exp(-jnp.abs(z)))
    lg_ref[...] = (ls / GATE_TAU).reshape(TILE_CHUNKS, CHUNK, dk)


def _gla_in(x, modc, g_pre, w_main, w_gz, w_gu, b_gate):
    nc = x.shape[0]
    dk = GLA_HEADS * DKH
    dv = GLA_HEADS * DVH
    tok = lambda w: pl.BlockSpec((TILE_CHUNKS, CHUNK, w), lambda i: (i, 0, 0))
    full = lambda a: pl.BlockSpec(a.shape, lambda i: (0,) * a.ndim)
    outs = [jax.ShapeDtypeStruct((nc, CHUNK, w), F32) for w in (dk, dk, dv, dv, dk)]
    return pl.pallas_call(
        _gla_in_kernel,
        grid=(nc // TILE_CHUNKS,),
        in_specs=[tok(D_MODEL), pl.BlockSpec((TILE_CHUNKS, 6, D_MODEL), lambda i: (i, 0, 0)),
                  full(g_pre), full(w_main), full(w_gz), full(w_gu), full(b_gate)],
        out_specs=[tok(dk), tok(dk), tok(dv), tok(dv), tok(dk)],
        out_shape=outs,
        compiler_params=_cparams(("arbitrary",)),
        name="gla_in",
    )(x, modc, g_pre, w_main, w_gz, w_gu, b_gate)


def _gla_core_kernel(*refs, nchunks):
    ns = GLA_SEQS
    q_refs, k_refs, v_refs, lg_refs = (refs[g * ns:(g + 1) * ns] for g in range(4))
    s0_ref = refs[4 * ns]
    o_refs = refs[4 * ns + 1:5 * ns + 1]
    sout_ref, s_scr = refs[5 * ns + 1], refs[5 * ns + 2]
    j = pl.program_id(1)

    @pl.when(j == 0)
    def _():
        s_scr[...] = s0_ref[...]

    nsub = CHUNK // SUB
    npair = SUB * SUB
    rr = lax.broadcasted_iota(jnp.int32, (CHUNK, CHUNK), 0)
    cc = lax.broadcasted_iota(jnp.int32, (CHUNK, CHUNK), 1)
    tril = (cc <= rr).astype(BF16)
    ones_k = jnp.ones((DKH, DKH), BF16)
    pr = lax.broadcasted_iota(jnp.int32, (nsub * npair, 2 * CHUNK), 0)
    pc = lax.broadcasted_iota(jnp.int32, (nsub * npair, 2 * CHUNK), 1)
    ps, pt, pi = pr % SUB, (pr // SUB) % SUB, pr // npair
    place = ((pc == SUB * pi + ps) & (ps <= pt)).astype(F32)
    st = lax.broadcasted_iota(jnp.int32, (CHUNK, nsub * npair), 0)
    sp = lax.broadcasted_iota(jnp.int32, (CHUNK, nsub * npair), 1)
    sel = (sp // SUB == st).astype(BF16)
    row_sub = lax.broadcasted_iota(jnp.int32, (CHUNK, DKH), 0) // SUB
    key_row = lax.broadcasted_iota(jnp.int32, (CHUNK, DKH), 0)
    zeros_v = jnp.zeros((CHUNK, DVH), BF16)

    for b in range(ns):
        lg = lg_refs[b][0]
        hi, mid, lo = _split3(lg)
        bcum = _dot(tril, hi) + _dot(tril, mid) + _dot(tril, lo)
        q_all = q_refs[b][0]
        k_all = k_refs[b][0]
        v_all = v_refs[b][0]
        heads = range(GLA_HEADS)
        qs = [q_all[:, h * DKH:(h + 1) * DKH] for h in heads]
        ks = [k_all[:, h * DKH:(h + 1) * DKH] for h in heads]
        bs_ = [bcum[:, h * DKH:(h + 1) * DKH] for h in heads]
        vbs = [v_all[:, h * DVH:(h + 1) * DVH].astype(BF16) for h in heads]
        outs = []
        for h in heads:
            blast = bs_[h][CHUNK - 1:CHUNK, :]
            s_t = s_scr[b, h]
            outs.append(_dot_nt((qs[h] * jnp.exp(bs_[h])).astype(BF16), s_t.astype(BF16)))
            kdec = (ks[h] * jnp.exp(blast - bs_[h])).astype(BF16)
            s_scr[b, h] = s_t * jnp.exp(blast) + _dot_tn(vbs[h], kdec)
        pairs = []
        for h in heads:
            qh, kh, bh = qs[h], ks[h], bs_[h]
            qe = jnp.concatenate([jnp.broadcast_to(qh[t:t + 1], (SUB, DKH)) for t in range(CHUNK)], axis=0)
            be = jnp.concatenate([jnp.broadcast_to(bh[t:t + 1], (SUB, DKH)) for t in range(CHUNK)], axis=0)
            kt = jnp.concatenate([kh[SUB * i:SUB * (i + 1)] for i in range(nsub) for _ in range(SUB)], axis=0)
            bt = jnp.concatenate([bh[SUB * i:SUB * (i + 1)] for i in range(nsub) for _ in range(SUB)], axis=0)
            pairs.append((qe * kt * jnp.exp(jnp.minimum(be - bt, 0.0))).astype(BF16))
        rsums = [_dot(pairs[h], ones_k) for h in heads]
        a_diag = [_dot(sel, (rsums[h] * place).astype(BF16)) for h in heads]
        a_off = []
        for h in heads:
            qh, kh, bh = qs[h], ks[h], bs_[h]
            qd, kd = [], []
            for i in range(1, nsub):
                anc = bh[SUB * i - 1:SUB * i, :]
                qd.append(jnp.where(row_sub == i, qh * jnp.exp(jnp.minimum(bh - anc, 0.0)), 0.0))
                kd.append(jnp.where(key_row < SUB * i, kh * jnp.exp(jnp.minimum(anc - bh, 0.0)), 0.0))
            qd = jnp.concatenate(qd, axis=-1).astype(BF16)
            kd = jnp.concatenate(kd, axis=-1).astype(BF16)
            a_off.append(_dot_nt(qd, jnp.concatenate([kd, jnp.zeros_like(kd)], axis=0)))
        for h in heads:
            a = (a_diag[h] + a_off[h]).astype(BF16)
            o = outs[h] + _dot(a, jnp.concatenate([vbs[h], zeros_v], axis=0))
            o_refs[b][0, :, h * DVH:(h + 1) * DVH] = o

    @pl.when(j == nchunks - 1)
    def _():
        sout_ref[...] = s_scr[...]


def _gla_core(q, k, v, lg, s0_t, *, chunk0, nblocks, nchunks):
    dk = GLA_HEADS * DKH
    dv = GLA_HEADS * DVH
    ns = GLA_SEQS

    def tok(w, g):
        return pl.BlockSpec((1, CHUNK, w), lambda s, j: (chunk0 + (s * ns + g) * nchunks + j, 0, 0))

    st_spec = pl.BlockSpec((ns, GLA_HEADS, DVH, DKH), lambda s, j: (s, 0, 0, 0))
    o_spec = pl.BlockSpec((1, CHUNK, dv), lambda s, j: (s * nchunks + j, 0, 0))
    in_specs = [tok(w, g) for w in (dk, dk, dv, dk) for g in range(ns)] + [st_spec]
    args = [a for a in (q, k, v, lg) for _ in range(ns)] + [s0_t]
    res = pl.pallas_call(
        functools.partial(_gla_core_kernel, nchunks=nchunks),
        grid=(nblocks, nchunks),
        in_specs=in_specs,
        out_specs=[o_spec] * ns + [st_spec],
        out_shape=[jax.ShapeDtypeStruct((nblocks * nchunks, CHUNK, dv), F32)] * ns
        + [jax.ShapeDtypeStruct((nblocks * ns, GLA_HEADS, DVH, DKH), F32)],
        scratch_shapes=[pltpu.VMEM((ns, GLA_HEADS, DVH, DKH), F32)],
        compiler_params=_cparams(("arbitrary", "arbitrary")),
        name="gla_core",
    )(*args)
    return res[:ns], res[ns]


def _gla_out_kernel(o_ref, r_ref, x_ref, mod_ref, gn_ref, w_ref, gpost_ref, x1_ref):
    o = o_ref[...].reshape(TILE, GLA_HEADS * DVH)
    r = r_ref[...].reshape(TILE, GLA_HEADS * DVH)
    gn = gn_ref[...]
    parts = [_rms(o[:, h * DVH:(h + 1) * DVH], gn) for h in range(GLA_HEADS)]
    y = jnp.concatenate(parts, axis=-1) * (r * _sigmoid(r))
    mix = _dot(y.astype(BF16), w_ref[...]).reshape(TILE_CHUNKS, CHUNK, D_MODEL)
    m = mod_ref[...]
    x1_ref[...] = x_ref[...] + m[:, 2:3, :] * _rms(mix, gpost_ref[...])


def _gla_out(o, r, x, modc, g_norm, w_out, g_post):
    nc = x.shape[0]
    tok = lambda w: pl.BlockSpec((TILE_CHUNKS, CHUNK, w), lambda i: (i, 0, 0))
    full = lambda a: pl.BlockSpec(a.shape, lambda i: (0,) * a.ndim)
    return pl.pallas_call(
        _gla_out_kernel,
        grid=(nc // TILE_CHUNKS,),
        in_specs=[tok(GLA_HEADS * DVH), tok(GLA_HEADS * DVH), tok(D_MODEL),
                  pl.BlockSpec((TILE_CHUNKS, 6, D_MODEL), lambda i: (i, 0, 0)),
                  full(g_norm), full(w_out), full(g_post)],
        out_specs=tok(D_MODEL),
        out_shape=jax.ShapeDtypeStruct(x.shape, F32),
        compiler_params=_cparams(("arbitrary",)),
        name="gla_out",
    )(o, r, x, modc, g_norm, w_out, g_post)


def _conv_kernel(x_ref, mod_ref, hist0_ref, gpre_ref, w1_ref, b1_ref, wdw_ref, bdw_ref, lng_ref, lnb_ref,
                 w2_ref, b2_ref, gpost_ref, x1_ref, hist_out_ref, ext_scr, shift_scr, *, nseq, ntiles):
    j = pl.program_id(1)
    tt = TILE // nseq

    @pl.when(j == 0)
    def _():
        for s in range(nseq):
            ext_scr[s, 0:HIST, :] = hist0_ref[s]

    m = mod_ref[...]
    h = _rms(x_ref[...], gpre_ref[...]) * (1.0 + m[:, 1:2, :]) + m[:, 0:1, :]
    p = _dot(h.reshape(TILE, D_MODEL).astype(BF16), w1_ref[...]) + b1_ref[...]
    u = p[:, :D_MODEL] * _sigmoid(p[:, D_MODEL:])
    first = HIST - (CONV_WIDTH - 1)
    ys = []
    span = tt + HIST - 8
    for s in range(nseq):
        ext_scr[s, HIST:HIST + tt, :] = u[s * tt:(s + 1) * tt]
        for d in range(1, 8):
            shift_scr[d - 1, :, :] = ext_scr[s, d:d + span, :]
        cols = []
        for c in range(D_MODEL // 128):
            lanes = slice(c * 128, (c + 1) * 128)
            acc = jnp.zeros((tt, 128), F32)
            for tap in range(CONV_WIDTH):
                off = first + tap
                if off % 8 == 0:
                    win = ext_scr[s, off:off + tt, lanes]
                else:
                    win = shift_scr[off % 8 - 1, off - off % 8:off - off % 8 + tt, lanes]
                acc = acc + win * wdw_ref[tap:tap + 1, lanes]
            cols.append(acc)
        ys.append(jnp.concatenate(cols, axis=-1))
        hist_new = ext_scr[s, tt:tt + HIST, :]
        ext_scr[s, 0:HIST, :] = hist_new

        @pl.when(j == ntiles - 1)
        def _():
            hist_out_ref[s] = hist_new
    y = jnp.concatenate(ys, axis=0) + bdw_ref[...]
    yc = y - jnp.mean(y, axis=-1, keepdims=True)
    yn = yc * lax.rsqrt(jnp.mean(yc * yc, axis=-1, keepdims=True) + EPS) * lng_ref[...] + lnb_ref[...]
    act = yn * _sigmoid(yn)
    mix = (_dot(act.astype(BF16), w2_ref[...]) + b2_ref[...]).reshape(TILE_CHUNKS, CHUNK, D_MODEL)
    x1_ref[...] = x_ref[...] + m[:, 2:3, :] * _rms(mix, gpost_ref[...])


def _conv(x, modc, hist0, params, *, chunk0, nseq_total, ntiles, nseq):
    tile0 = chunk0 // TILE_CHUNKS
    if ntiles == 1:
        cidx = lambda s, j: (tile0 + s, 0, 0)
        oidx = lambda s, j: (s, 0, 0)
    else:
        assert nseq == 1
        cidx = lambda s, j: (tile0 + s * ntiles + j, 0, 0)
        oidx = lambda s, j: (s * ntiles + j, 0, 0)
    full = lambda a: pl.BlockSpec(a.shape, lambda s, j: (0,) * a.ndim)
    hist_spec = pl.BlockSpec((nseq, HIST, D_MODEL), lambda s, j: (s, 0, 0))
    nblocks = nseq_total // nseq
    return pl.pallas_call(
        functools.partial(_conv_kernel, nseq=nseq, ntiles=ntiles),
        grid=(nblocks, ntiles),
        in_specs=[pl.BlockSpec((TILE_CHUNKS, CHUNK, D_MODEL), cidx),
                  pl.BlockSpec((TILE_CHUNKS, 6, D_MODEL), cidx),
                  hist_spec] + [full(a) for a in params],
        out_specs=[pl.BlockSpec((TILE_CHUNKS, CHUNK, D_MODEL), oidx), hist_spec],
        out_shape=[jax.ShapeDtypeStruct((nblocks * ntiles * TILE_CHUNKS, CHUNK, D_MODEL), F32),
                   jax.ShapeDtypeStruct((nseq_total, HIST, D_MODEL), F32)],
        scratch_shapes=[pltpu.VMEM((nseq, HIST + TILE // nseq, D_MODEL), F32),
                        pltpu.VMEM((7, TILE // nseq + HIST - 8, D_MODEL), F32)],
        compiler_params=_cparams(("arbitrary", "arbitrary")),
        name="conv",
    )(x, modc, hist0, *params)


def _moe_input(x_ref, mod_ref, g_ref):
    m = mod_ref[...]
    return _rms(x_ref[...], g_ref[...]) * (1.0 + m[:, 4:5, :]) + m[:, 3:4, :]


def _router_kernel(x_ref, mod_ref, g_ref, wrt_ref, br_ref,
                   idx_ref, gate_ref, rank_ref, cnt_ref, carry_scr):
    i = pl.program_id(0)

    @pl.when(i == 0)
    def _():
        carry_scr[...] = jnp.zeros_like(carry_scr)

    h2 = _moe_input(x_ref, mod_ref, g_ref).reshape(TILE, D_MODEL)
    h_hi = h2.astype(BF16)
    h_lo = (h2 - h_hi.astype(F32)).astype(BF16)
    w = wrt_ref[...]
    w_hi = w.astype(BF16)
    w_lo = (w - w_hi.astype(F32)).astype(BF16)
    logits = _dot_nt(w_hi, h_hi) + _dot_nt(w_lo, h_hi) + _dot_nt(w_hi, h_lo) + br_ref[...]
    e_iota = lax.broadcasted_iota(jnp.int32, (N_EXPERTS, TILE), 0)
    vals, idxs, sels = [], [], []
    lt = logits
    for _ in range(TOP_K):
        mx = jnp.max(lt, axis=0, keepdims=True)
        ix = jnp.min(jnp.where(lt == mx, e_iota, N_EXPERTS), axis=0, keepdims=True)
        sl = e_iota == ix
        vals.append(mx)
        idxs.append(ix)
        sels.append(sl)
        lt = jnp.where(sl, -jnp.inf, lt)
    ex = [jnp.exp(v - vals[0]) for v in vals]
    den = ex[0] + ex[1] + ex[2] + ex[3]
    gate_ref[...] = _stack_rows([e / den for e in ex])
    idx_ref[...] = _stack_rows(idxs)
    chosen = (sels[0] | sels[1] | sels[2] | sels[3])
    chosen_f = jnp.where(chosen, 1.0, 0.0)
    rr = lax.broadcasted_iota(jnp.int32, (TILE, TILE), 0)
    cc = lax.broadcasted_iota(jnp.int32, (TILE, TILE), 1)
    before = (rr < cc).astype(BF16)
    pos = carry_scr[...][:, 0:1] + _dot(chosen_f.astype(BF16), before)
    ranks = [jnp.sum(jnp.where(sl, pos, 0.0), axis=0, keepdims=True) for sl in sels]
    rank_ref[...] = _stack_rows(ranks).astype(jnp.int32)
    carry_scr[...] = carry_scr[...] + jnp.sum(chosen_f, axis=1, keepdims=True)
    cnt_ref[...] = carry_scr[...]


def _router(x, modc, g_pre, w_router_t, b_router):
    nc = x.shape[0]
    nt = nc // TILE_CHUNKS
    n = nc * CHUNK
    full = lambda a: pl.BlockSpec(a.shape, lambda i: (0,) * a.ndim)
    tok = pl.BlockSpec((TILE_CHUNKS, CHUNK, D_MODEL), lambda i: (i, 0, 0))
    kt = pl.BlockSpec((TOP_K, TILE), lambda i: (0, i))
    return pl.pallas_call(
        _router_kernel,
        grid=(nt,),
        in_specs=[tok, pl.BlockSpec((TILE_CHUNKS, 6, D_MODEL), lambda i: (i, 0, 0)),
                  full(g_pre), full(w_router_t), full(b_router)],
        out_specs=[kt, kt, kt, pl.BlockSpec((N_EXPERTS, 128), lambda i: (0, 0))],
        out_shape=[jax.ShapeDtypeStruct((TOP_K, n), jnp.int32),
                   jax.ShapeDtypeStruct((TOP_K, n), F32),
                   jax.ShapeDtypeStruct((TOP_K, n), jnp.int32),
                   jax.ShapeDtypeStruct((N_EXPERTS, 128), F32)],
        scratch_shapes=[pltpu.VMEM((N_EXPERTS, 128), F32)],
        compiler_params=_cparams(("arbitrary",)),
        name="router",
    )(x, modc, g_pre, w_router_t, b_router)


def _row_copy(src, dst, sem):
    return pltpu.make_async_copy(src, dst, sem)


def _dispatch_kernel(dest_ref, pend_ref, x_ref, mod_ref, g_ref, rows_ref, h_scr, zero_scr, sem, zsem):
    i = pl.program_id(0)
    nt = pl.num_programs(0)
    slot = i % 2

    @pl.when(i == 0)
    def _():
        zero_scr[...] = jnp.zeros_like(zero_scr)

        n_blocks = rows_ref.shape[0] // EXPERT_ROWS
        total = pend_ref[N_EXPERTS - 1]

        def block_to_clear(e):
            is_tail = e >= N_EXPERTS
            ee = jnp.minimum(e, N_EXPERTS - 1)
            end = pend_ref[ee]
            start = jnp.where(ee == 0, 0, pend_ref[jnp.maximum(ee - 1, 0)])
            tail_row = total + (e - N_EXPERTS) * EXPERT_ROWS
            row0 = jnp.where(is_tail, tail_row, jnp.maximum(end - EXPERT_ROWS, 0))
            needed = jnp.where(is_tail, tail_row < n_blocks * EXPERT_ROWS, end > start)
            row0 = jnp.minimum(row0, (n_blocks - 1) * EXPERT_ROWS)
            dst = rows_ref.at[pl.ds(pl.multiple_of(row0, EXPERT_ROWS), EXPERT_ROWS)]
            return needed, _row_copy(zero_scr, dst, zsem)

        def clear(e, c):
            needed, cp = block_to_clear(e)

            @pl.when(needed)
            def _():
                cp.start()
            return c

        def settle(e, c):
            needed, cp = block_to_clear(e)

            @pl.when(needed)
            def _():
                cp.wait()
            return c

        lax.fori_loop(0, 2 * N_EXPERTS, clear, 0)
        lax.fori_loop(0, 2 * N_EXPERTS, settle, 0)

    h_scr[slot] = _moe_input(x_ref, mod_ref, g_ref).reshape(TILE, D_MODEL)
    base = i * (TOP_K * TILE)

    def issue(r, c):
        for k in range(TOP_K):
            d = dest_ref[base + k * TILE + r]
            _row_copy(h_scr.at[slot, pl.ds(r, 1)], rows_ref.at[pl.ds(d, 1)], sem.at[slot]).start(priority=k % 2)
        return c

    lax.fori_loop(0, TILE, issue, 0, unroll=8)

    def drain(sl):
        def body(r, c):
            for k in range(TOP_K):
                _row_copy(h_scr.at[sl, pl.ds(0, 1)], rows_ref.at[pl.ds(0, 1)], sem.at[sl]).wait()
            return c
        lax.fori_loop(0, TILE, body, 0, unroll=8)

    @pl.when(i > 0)
    def _():
        drain(1 - slot)

    @pl.when(i == nt - 1)
    def _():
        drain(slot)


def _dispatch(dest_flat, p_ends, x, modc, g_pre, n_rows):
    nc = x.shape[0]
    return pl.pallas_call(
        _dispatch_kernel,
        grid_spec=pltpu.PrefetchScalarGridSpec(
            num_scalar_prefetch=2,
            grid=(nc // TILE_CHUNKS,),
            in_specs=[pl.BlockSpec((TILE_CHUNKS, CHUNK, D_MODEL), lambda i, dest, pend: (i, 0, 0)),
                      pl.BlockSpec((TILE_CHUNKS, 6, D_MODEL), lambda i, dest, pend: (i, 0, 0)),
                      pl.BlockSpec(g_pre.shape, lambda i, dest, pend: (0, 0))],
            out_specs=pl.BlockSpec(memory_space=pl.ANY),
            scratch_shapes=[pltpu.VMEM((2, TILE, D_MODEL), F32), pltpu.VMEM((EXPERT_ROWS, D_MODEL), F32),
                            pltpu.SemaphoreType.DMA((2,)), pltpu.SemaphoreType.DMA],
        ),
        out_shape=jax.ShapeDtypeStruct((n_rows, D_MODEL), F32),
        compiler_params=_cparams(("arbitrary",)),
        name="dispatch",
    )(dest_flat, p_ends, x, modc, g_pre)


def _experts_kernel(be_ref, nu_ref, run_ref, nxt_ref, x_ref, b1_ref, b2_ref, w1_hbm, w2_hbm, y_ref,
                    w1_f32, w2_f32, w1_scr, w2_scr, sem, *, layer):
    i = pl.program_id(0)
    valid = i < nu_ref[0]
    e = be_ref[i]
    changed = jnp.logical_or(i == 0, e != be_ref[jnp.maximum(i - 1, 0)])
    slot = run_ref[i] % 2
    dff = w2_scr.shape[0]

    def fetch(expert, sl):
        return (_row_copy(w1_hbm.at[layer, expert], w1_f32.at[sl], sem.at[sl, 0]),
                _row_copy(w2_hbm.at[layer, expert], w2_f32.at[sl], sem.at[sl, 1]))

    @pl.when(i == 0)
    def _():
        for cp in fetch(e, 0):
            cp.start()

    @pl.when(jnp.logical_and(valid, changed))
    def _():
        nxt = nxt_ref[i]

        @pl.when(nxt >= 0)
        def _():
            for cp in fetch(nxt, 1 - slot):
                cp.start()

        for cp in fetch(e, slot):
            cp.wait()
        rows = 128

        def cast(c, carry):
            r0 = pl.multiple_of(c * rows, rows)
            w1_scr[pl.ds(r0, rows), :] = w1_f32[slot, pl.ds(r0, rows), :].astype(BF16)
            w2_scr[pl.ds(r0, rows), :] = w2_f32[slot, pl.ds(r0, rows), :].astype(BF16)
            return carry

        lax.fori_loop(0, D_MODEL // rows, cast, 0)

    @pl.when(valid)
    def _():
        h = _dot(x_ref[...].astype(BF16), w1_scr[...]) + b1_ref[0, 0]
        gate = jnp.minimum(h[:, :dff], SWIGLU_LIMIT)
        up = jnp.clip(h[:, dff:], -SWIGLU_LIMIT, SWIGLU_LIMIT)
        act = (up + 1.0) * (gate * _sigmoid(SWIGLU_ALPHA * gate))
        y_ref[...] = _dot(act.astype(BF16), w2_scr[...]) + b2_ref[0, 0]

    @pl.when(jnp.logical_not(valid))
    def _():
        y_ref[...] = jnp.zeros_like(y_ref)


def _experts(blk_e, n_used, blk_run, blk_next, x_rows, w1, b1, w2, b2, layer):
    n_rows = x_rows.shape[0]
    n_blocks = n_rows // EXPERT_ROWS
    nl, e, d, f2 = w1.shape
    dff = w2.shape[2]
    assert d == dff == D_MODEL
    row_idx = lambda i, be, nu, run, nxt: (jnp.minimum(i, nu[0] - 1), 0)
    b_idx = lambda i, be, nu, run, nxt: (layer, be[i], 0, 0)
    return pl.pallas_call(
        functools.partial(_experts_kernel, layer=layer),
        grid_spec=pltpu.PrefetchScalarGridSpec(
            num_scalar_prefetch=4,
            grid=(n_blocks,),
            in_specs=[pl.BlockSpec((EXPERT_ROWS, D_MODEL), row_idx),
                      pl.BlockSpec((1, 1, 1, f2), b_idx),
                      pl.BlockSpec((1, 1, 1, D_MODEL), b_idx),
                      pl.BlockSpec(memory_space=pl.ANY),
                      pl.BlockSpec(memory_space=pl.ANY)],
            out_specs=pl.BlockSpec((EXPERT_ROWS, D_MODEL), lambda i, be, nu, run, nxt: (i, 0)),
            scratch_shapes=[pltpu.VMEM((2, d, f2), F32), pltpu.VMEM((2, dff, D_MODEL), F32),
                            pltpu.VMEM((d, f2), BF16), pltpu.VMEM((dff, D_MODEL), BF16),
                            pltpu.SemaphoreType.DMA((2, 2))],
        ),
        out_shape=jax.ShapeDtypeStruct((n_rows, D_MODEL), F32),
        compiler_params=_cparams(("arbitrary",), EXPERTS_VMEM_LIMIT),
        name="experts",
    )(blk_e, n_used, blk_run, blk_next, x_rows, b1.reshape(nl, e, 1, f2), b2.reshape(nl, e, 1, D_MODEL), w1, w2)


def _combine_kernel(dest_ref, gate_ref, x_ref, mod_ref, gpost_ref, rows_ref, o_ref, ybuf, sem):
    i = pl.program_id(0)
    nt = pl.num_programs(0)
    slot = i % 2

    def gather_tile(tile, sl):
        base = tile * (TOP_K * TILE)

        def issue(r, c):
            for k in range(TOP_K):
                d = dest_ref[base + k * TILE + r]
                _row_copy(rows_ref.at[pl.ds(d, 1)], ybuf.at[sl, k, pl.ds(r, 1)], sem.at[sl]).start(priority=k % 2)
            return c

        lax.fori_loop(0, TILE, issue, 0, unroll=8)

    @pl.when(i == 0)
    def _():
        gather_tile(0, 0)

    @pl.when(i + 1 < nt)
    def _():
        gather_tile(i + 1, 1 - slot)

    def drain(r, c):
        for k in range(TOP_K):
            _row_copy(rows_ref.at[pl.ds(0, 1)], ybuf.at[slot, 0, pl.ds(0, 1)], sem.at[slot]).wait()
        return c

    lax.fori_loop(0, TILE, drain, 0, unroll=8)

    g = gate_ref[...]
    y = g[:, 0:1] * ybuf[slot, 0]
    for k in range(1, TOP_K):
        y = y + g[:, k:k + 1] * ybuf[slot, k]
    m = mod_ref[...]
    y = y.reshape(TILE_CHUNKS, CHUNK, D_MODEL)
    o_ref[...] = x_ref[...] + m[:, 5:6, :] * _rms(y, gpost_ref[...])


def _combine(dest_flat, gates, x, modc, g_post, y_rows):
    nc = x.shape[0]
    tok = pl.BlockSpec((TILE_CHUNKS, CHUNK, D_MODEL), lambda i, dest: (i, 0, 0))
    return pl.pallas_call(
        _combine_kernel,
        grid_spec=pltpu.PrefetchScalarGridSpec(
            num_scalar_prefetch=1,
            grid=(nc // TILE_CHUNKS,),
            in_specs=[pl.BlockSpec((TILE, TOP_K), lambda i, dest: (i, 0)),
                      tok,
                      pl.BlockSpec((TILE_CHUNKS, 6, D_MODEL), lambda i, dest: (i, 0, 0)),
                      pl.BlockSpec(g_post.shape, lambda i, dest: (0, 0)),
                      pl.BlockSpec(memory_space=pl.ANY)],
            out_specs=tok,
            scratch_shapes=[pltpu.VMEM((2, TOP_K, TILE, D_MODEL), F32), pltpu.SemaphoreType.DMA((2,))],
        ),
        out_shape=jax.ShapeDtypeStruct(x.shape, F32),
        compiler_params=_cparams(("arbitrary",)),
        name="combine",
    )(dest_flat, gates, x, modc, g_post, y_rows)


def _moe_layer(x, modc, g_pre, g_post, w_router, b_router, w1, b1, w2, b2, layer):
    nc = x.shape[0]
    n = nc * CHUNK
    nt = n // TILE
    idx_t, gate_t, rank_t, cnt = _router(x, modc, g_pre, w_router.T, b_router.reshape(N_EXPERTS, 1))
    counts = cnt[:, 0].astype(jnp.int32)
    padded = (counts + EXPERT_ROWS - 1) // EXPERT_ROWS * EXPERT_ROWS
    p_ends = jnp.cumsum(padded).astype(jnp.int32)
    p_starts = p_ends - padded
    n_blocks = -(-(n * TOP_K) // EXPERT_ROWS) + N_EXPERTS
    n_used = p_ends[-1] // EXPERT_ROWS
    blk = jnp.minimum(jnp.arange(n_blocks, dtype=jnp.int32), n_used - 1) * EXPERT_ROWS
    blk_e = jnp.minimum(jnp.sum((p_ends[None, :] <= blk[:, None]).astype(jnp.int32), axis=1), N_EXPERTS - 1)
    eids = jnp.arange(N_EXPERTS, dtype=jnp.int32)
    start_of = jnp.sum(jnp.where(idx_t[:, :, None] == eids, p_starts, 0), axis=-1)
    dest_t = start_of + rank_t
    dest_flat = dest_t.reshape(TOP_K, nt, TILE).transpose(1, 0, 2).reshape(-1)
    blk_run = jnp.cumsum(jnp.concatenate([jnp.ones((1,), jnp.int32),
                                          (blk_e[1:] != blk_e[:-1]).astype(jnp.int32)])) - 1
    later_used = (eids[None, :] > eids[:, None]) & (counts > 0)[None, :]
    next_used = jnp.min(jnp.where(later_used, eids[None, :], N_EXPERTS), axis=1)
    next_used = jnp.where(next_used == N_EXPERTS, -1, next_used)
    blk_next = jnp.sum(jnp.where(blk_e[:, None] == eids, next_used, 0), axis=1)
    x_rows = _dispatch(dest_flat, p_ends, x, modc, g_pre, n_blocks * EXPERT_ROWS)
    y_rows = _experts(blk_e, n_used.reshape(1), blk_run, blk_next, x_rows, w1, b1, w2, b2, layer)
    return _combine(dest_flat, gate_t.T, x, modc, g_post, y_rows)


def kernel(x_prompt, x_sample, state_gla, cache_conv, c_prompt, c_sample, w_ada, b_ada, norm_pre, norm_post, w_gla_in, w_gla_gate_up, b_gla_gate, g_gla_norm, w_gla_out, w_pw1, b_pw1, w_dw, b_dw, ln_g, ln_b, w_pw2, b_pw2, w_router, b_router, w_e1, b_e1, w_e2, b_e2):
    bp, lp, d = x_prompt.shape
    bs, ls, _ = x_sample.shape
    depth = w_ada.shape[0]
    assert d == D_MODEL and ls == CHUNK and lp % TILE == 0 and bs % TILE_CHUNKS == 0
    assert bp == GLA_SEQS and bs % GLA_SEQS == 0
    ncp = bp * lp // CHUNK
    ncs = bs
    nseq = bp + bs
    nch = lp // CHUNK

    x = jnp.concatenate([x_prompt.reshape(ncp, CHUNK, d), x_sample.reshape(ncs, CHUNK, d)], axis=0)
    rows = -(-nseq // 8) * 8
    c_all = jnp.concatenate([c_prompt, c_sample, jnp.zeros((rows - nseq, d), F32)], axis=0)
    mod = _ada(c_all, w_ada, b_ada)
    mod_p = jnp.broadcast_to(mod[:, :bp, None, :], (depth, bp, nch, 6 * d)).reshape(depth, ncp, 6 * d)
    modc = jnp.concatenate([mod_p, mod[:, bp:nseq]], axis=1).reshape(depth, ncp + ncs, 6, d)

    row = lambda a: a.reshape(1, -1)
    dk = GLA_HEADS * DKH
    dv = GLA_HEADS * DVH
    gla_states, conv_hists = [], []
    for i in range(depth):
        j = i // 2
        if i % 2 == 0:
            w_in = w_gla_in[j]
            rank = w_gla_gate_up.shape[1]
            w_main = w_in[:, :2 * dk + 2 * dv].astype(BF16)
            w_gz = jnp.pad(w_in[:, 2 * dk + 2 * dv:], ((0, 0), (0, 128 - rank))).astype(BF16)
            w_gu = jnp.pad(w_gla_gate_up[j], ((0, 128 - rank), (0, 0))).astype(BF16)
            q, k, v, r, lg = _gla_in(x, modc[i], row(norm_pre[i, 0]), w_main, w_gz, w_gu, row(b_gla_gate[j]))
            s0_p = jnp.zeros((bp, GLA_HEADS, DVH, DKH), F32)
            s0_s = jnp.swapaxes(state_gla[j], -1, -2)
            o_p, st_p = _gla_core(q, k, v, lg, s0_p, chunk0=0, nblocks=1, nchunks=nch)
            o_s, st_s = _gla_core(q, k, v, lg, s0_s, chunk0=ncp, nblocks=bs // GLA_SEQS, nchunks=1)
            o_s = jnp.stack(o_s, axis=1).reshape(ncs, CHUNK, dv)
            o = jnp.concatenate(list(o_p) + [o_s], axis=0)
            gla_states.append((jnp.swapaxes(st_p, -1, -2), jnp.swapaxes(st_s, -1, -2)))
            x = _gla_out(o, r, x, modc[i], row(g_gla_norm[j]), w_gla_out[j].astype(BF16), row(norm_post[i, 0]))
        else:
            params = (row(norm_pre[i, 0]), w_pw1[j].astype(BF16), row(b_pw1[j]), w_dw[j], row(b_dw[j]),
                      row(ln_g[j]), row(ln_b[j]), w_pw2[j].astype(BF16), row(b_pw2[j]), row(norm_post[i, 0]))
            keep = CONV_WIDTH - 1
            h0_p = jnp.zeros((bp, HIST, d), F32)
            h0_s = jnp.pad(cache_conv[j], ((0, 0), (HIST - keep, 0), (0, 0)))
            x_p, hp = _conv(x, modc[i], h0_p, params, chunk0=0, nseq_total=bp, ntiles=lp // TILE, nseq=1)
            x_s, hs = _conv(x, modc[i], h0_s, params, chunk0=ncp, nseq_total=bs, ntiles=1, nseq=TILE_CHUNKS)
            x = jnp.concatenate([x_p, x_s], axis=0)
            conv_hists.append((hp[:, HIST - keep:], hs[:, HIST - keep:]))
        x = _moe_layer(x, modc[i], row(norm_pre[i, 1]), row(norm_post[i, 1]),
                       w_router[i], b_router[i], w_e1, b_e1, w_e2, b_e2, i)

    y_prompt = x[:ncp].reshape(bp, lp, d)
    y_sample = x[ncp:].reshape(bs, ls, d)
    gla_p = jnp.stack([s[0] for s in gla_states])
    gla_s = jnp.stack([s[1] for s in gla_states])
    conv_p = jnp.stack([c[0] for c in conv_hists])
    conv_s = jnp.stack([c[1] for c in conv_hists])
    return (y_prompt, y_sample, gla_p, gla_s, conv_p, conv_s)
```

```python
import functools

import jax
import jax.numpy as jnp
from jax import lax
from jax.experimental import pallas as pl
from jax.experimental.pallas import tpu as pltpu

F32 = jnp.float32
BF16 = jnp.bfloat16

D_MODEL = 1024
CHUNK = 64
TILE_CHUNKS = 4
TILE = CHUNK * TILE_CHUNKS
GLA_HEADS = 4
DKH = 128
DVH = 256
SUB = 16
GLA_SEQS = 4
GATE_TAU = 16.0
CONV_WIDTH = 31
HIST = 32
N_EXPERTS = 32
TOP_K = 4
SWIGLU_LIMIT = 7.0
SWIGLU_ALPHA = 1.702
EXPERT_ROWS = 256
ROW_CHUNK = 8
GROUP_ROWS = TILE * TOP_K + N_EXPERTS * ROW_CHUNK
CHUNK_SLOTS = 256
EPS = 1e-6
VMEM_LIMIT = 48 * 1024 * 1024
EXPERTS_VMEM_LIMIT = 56 * 1024 * 1024


def _cparams(sem, vmem=VMEM_LIMIT):
    return pltpu.CompilerParams(dimension_semantics=sem, vmem_limit_bytes=vmem)


def _dot(a, b):
    return jnp.dot(a, b, preferred_element_type=F32)


def _dot_nt(a, b):
    return lax.dot_general(a, b, (((1,), (1,)), ((), ())), preferred_element_type=F32)


def _dot_tn(a, b):
    return lax.dot_general(a, b, (((0,), (0,)), ((), ())), preferred_element_type=F32)


def _sigmoid(x):
    return 1.0 / (1.0 + jnp.exp(-x))


def _rms(x, g):
    return x * lax.rsqrt(jnp.mean(x * x, axis=-1, keepdims=True) + EPS) * g


def _stack_rows(rows):
    n = rows[0].shape[1]
    ri = lax.broadcasted_iota(jnp.int32, (len(rows), n), 0)
    out = jnp.broadcast_to(rows[0], (len(rows), n))
    for j in range(1, len(rows)):
        out = jnp.where(ri == j, jnp.broadcast_to(rows[j], (len(rows), n)), out)
    return out


def _split3(x):
    hi = x.astype(BF16)
    r1 = x - hi.astype(F32)
    mid = r1.astype(BF16)
    lo = (r1 - mid.astype(F32)).astype(BF16)
    return hi, mid, lo


def _ada_kernel(c_ref, w_ref, b_ref, o_ref):
    c = c_ref[...]
    cs = (c * _sigmoid(c)).astype(BF16)
    o_ref[0] = _dot(cs, w_ref[0].astype(BF16)) + b_ref[0]


def _ada(c_all, w_ada, b_ada):
    depth, d, n6 = w_ada.shape
    rows = c_all.shape[0]
    nt = n6 // d
    return pl.pallas_call(
        _ada_kernel,
        grid=(depth, nt),
        in_specs=[
            pl.BlockSpec((rows, d), lambda i, n: (0, 0)),
            pl.BlockSpec((1, d, d), lambda i, n: (i, 0, n)),
            pl.BlockSpec((1, 1, d), lambda i, n: (i, 0, n)),
        ],
        out_specs=pl.BlockSpec((1, rows, d), lambda i, n: (i, 0, n)),
        out_shape=jax.ShapeDtypeStruct((depth, rows, n6), F32),
        compiler_params=_cparams(("arbitrary", "arbitrary")),
        name="ada",
    )(c_all, w_ada, b_ada.reshape(depth, 1, n6))


def _gla_in_kernel(x_ref, mod_ref, g_ref, w_ref, wgz_ref, wgu_ref, bg_ref,
                   q_ref, k_ref, v_ref, r_ref, lg_ref):
    m = mod_ref[...]
    h = _rms(x_ref[...], g_ref[...]) * (1.0 + m[:, 1:2, :]) + m[:, 0:1, :]
    hb = h.reshape(TILE, D_MODEL).astype(BF16)
    p = _dot(hb, w_ref[...])
    dk = GLA_HEADS * DKH
    dv = GLA_HEADS * DVH
    q_ref[...] = (p[:, :dk] * (DKH ** -0.5)).reshape(TILE_CHUNKS, CHUNK, dk)
    k_ref[...] = p[:, dk:2 * dk].reshape(TILE_CHUNKS, CHUNK, dk)
    v_ref[...] = p[:, 2 * dk:2 * dk + dv].reshape(TILE_CHUNKS, CHUNK, dv)
    r_ref[...] = p[:, 2 * dk + dv:].reshape(TILE_CHUNKS, CHUNK, dv)
    gz = _dot(hb, wgz_ref[...])
    z = _dot(gz.astype(BF16), wgu_ref[...]) + bg_ref[...]
    ls = jnp.minimum(z, 0.0) - jnp.log(1.0 + jnp.exp(-jnp.abs(z)))
    lg_ref[...] = (ls / GATE_TAU).reshape(TILE_CHUNKS, CHUNK, dk)


def _gla_in(x, modc, g_pre, w_main, w_gz, w_gu, b_gate):
    nc = x.shape[0]
    dk = GLA_HEADS * DKH
    dv = GLA_HEADS * DVH
    tok = lambda w: pl.BlockSpec((TILE_CHUNKS, CHUNK, w), lambda i: (i, 0, 0))
    full = lambda a: pl.BlockSpec(a.shape, lambda i: (0,) * a.ndim)
    outs = [jax.ShapeDtypeStruct((nc, CHUNK, w), F32) for w in (dk, dk, dv, dv, dk)]
    return pl.pallas_call(
        _gla_in_kernel,
        grid=(nc // TILE_CHUNKS,),
        in_specs=[tok(D_MODEL), pl.BlockSpec((TILE_CHUNKS, 6, D_MODEL), lambda i: (i, 0, 0)),
                  full(g_pre), full(w_main), full(w_gz), full(w_gu), full(b_gate)],
        out_specs=[tok(dk), tok(dk), tok(dv), tok(dv), tok(dk)],
        out_shape=outs,
        compiler_params=_cparams(("arbitrary",)),
        name="gla_in",
    )(x, modc, g_pre, w_main, w_gz, w_gu, b_gate)


def _gla_core_kernel(*refs, nchunks):
    ns = GLA_SEQS
    q_refs, k_refs, v_refs, lg_refs = (refs[g * ns:(g + 1) * ns] for g in range(4))
    s0_ref = refs[4 * ns]
    o_refs = refs[4 * ns + 1:5 * ns + 1]
    sout_ref, s_scr = refs[5 * ns + 1], refs[5 * ns + 2]
    j = pl.program_id(1)

    @pl.when(j == 0)
    def _():
        s_scr[...] = s0_ref[...]

    nsub = CHUNK // SUB
    npair = SUB * SUB
    rr = lax.broadcasted_iota(jnp.int32, (CHUNK, CHUNK), 0)
    cc = lax.broadcasted_iota(jnp.int32, (CHUNK, CHUNK), 1)
    tril = (cc <= rr).astype(BF16)
    ones_k = jnp.ones((DKH, DKH), BF16)
    pr = lax.broadcasted_iota(jnp.int32, (nsub * npair, 2 * CHUNK), 0)
    pc = lax.broadcasted_iota(jnp.int32, (nsub * npair, 2 * CHUNK), 1)
    ps, pt, pi = pr % SUB, (pr // SUB) % SUB, pr // npair
    place = ((pc == SUB * pi + ps) & (ps <= pt)).astype(F32)
    st = lax.broadcasted_iota(jnp.int32, (CHUNK, nsub * npair), 0)
    sp = lax.broadcasted_iota(jnp.int32, (CHUNK, nsub * npair), 1)
    sel = (sp // SUB == st).astype(BF16)
    row_sub = lax.broadcasted_iota(jnp.int32, (CHUNK, DKH), 0) // SUB
    key_row = lax.broadcasted_iota(jnp.int32, (CHUNK, DKH), 0)
    zeros_v = jnp.zeros((CHUNK, DVH), BF16)

    for b in range(ns):
        lg = lg_refs[b][0]
        hi, mid, lo = _split3(lg)
        bcum = _dot(tril, hi) + _dot(tril, mid) + _dot(tril, lo)
        q_all = q_refs[b][0]
        k_all = k_refs[b][0]
        v_all = v_refs[b][0]
        heads = range(GLA_HEADS)
        qs = [q_all[:, h * DKH:(h + 1) * DKH] for h in heads]
        ks = [k_all[:, h * DKH:(h + 1) * DKH] for h in heads]
        bs_ = [bcum[:, h * DKH:(h + 1) * DKH] for h in heads]
        vbs = [v_all[:, h * DVH:(h + 1) * DVH].astype(BF16) for h in heads]
        outs = []
        for h in heads:
            blast = bs_[h][CHUNK - 1:CHUNK, :]
            s_t = s_scr[b, h]
            outs.append(_dot_nt((qs[h] * jnp.exp(bs_[h])).astype(BF16), s_t.astype(BF16)))
            kdec = (ks[h] * jnp.exp(blast - bs_[h])).astype(BF16)
            s_scr[b, h] = s_t * jnp.exp(blast) + _dot_tn(vbs[h], kdec)
        pairs = []
        for h in heads:
            qh, kh, bh = qs[h], ks[h], bs_[h]
            qe = jnp.concatenate([jnp.broadcast_to(qh[t:t + 1], (SUB, DKH)) for t in range(CHUNK)], axis=0)
            be = jnp.concatenate([jnp.broadcast_to(bh[t:t + 1], (SUB, DKH)) for t in range(CHUNK)], axis=0)
            kt = jnp.concatenate([kh[SUB * i:SUB * (i + 1)] for i in range(nsub) for _ in range(SUB)], axis=0)
            bt = jnp.concatenate([bh[SUB * i:SUB * (i + 1)] for i in range(nsub) for _ in range(SUB)], axis=0)
            pairs.append((qe * kt * jnp.exp(jnp.minimum(be - bt, 0.0))).astype(BF16))
        rsums = [_dot(pairs[h], ones_k) for h in heads]
        a_diag = [_dot(sel, (rsums[h] * place).astype(BF16)) for h in heads]
        a_off = []
        for h in heads:
            qh, kh, bh = qs[h], ks[h], bs_[h]
            qd, kd = [], []
            for i in range(1, nsub):
                anc = bh[SUB * i - 1:SUB * i, :]
                qd.append(jnp.where(row_sub == i, qh * jnp.exp(jnp.minimum(bh - anc, 0.0)), 0.0))
                kd.append(jnp.where(key_row < SUB * i, kh * jnp.exp(jnp.minimum(anc - bh, 0.0)), 0.0))
            qd = jnp.concatenate(qd, axis=-1).astype(BF16)
            kd = jnp.concatenate(kd, axis=-1).astype(BF16)
            a_off.append(_dot_nt(qd, jnp.concatenate([kd, jnp.zeros_like(kd)], axis=0)))
        for h in heads:
            a = (a_diag[h] + a_off[h]).astype(BF16)
            o = outs[h] + _dot(a, jnp.concatenate([vbs[h], zeros_v], axis=0))
            o_refs[b][0, :, h * DVH:(h + 1) * DVH] = o

    @pl.when(j == nchunks - 1)
    def _():
        sout_ref[...] = s_scr[...]


def _gla_core(q, k, v, lg, s0_t, *, chunk0, nblocks, nchunks):
    dk = GLA_HEADS * DKH
    dv = GLA_HEADS * DVH
    ns = GLA_SEQS

    def tok(w, g):
        return pl.BlockSpec((1, CHUNK, w), lambda s, j: (chunk0 + (s * ns + g) * nchunks + j, 0, 0))

    st_spec = pl.BlockSpec((ns, GLA_HEADS, DVH, DKH), lambda s, j: (s, 0, 0, 0))
    o_spec = pl.BlockSpec((1, CHUNK, dv), lambda s, j: (s * nchunks + j, 0, 0))
    in_specs = [tok(w, g) for w in (dk, dk, dv, dk) for g in range(ns)] + [st_spec]
    args = [a for a in (q, k, v, lg) for _ in range(ns)] + [s0_t]
    res = pl.pallas_call(
        functools.partial(_gla_core_kernel, nchunks=nchunks),
        grid=(nblocks, nchunks),
        in_specs=in_specs,
        out_specs=[o_spec] * ns + [st_spec],
        out_shape=[jax.ShapeDtypeStruct((nblocks * nchunks, CHUNK, dv), F32)] * ns
        + [jax.ShapeDtypeStruct((nblocks * ns, GLA_HEADS, DVH, DKH), F32)],
        scratch_shapes=[pltpu.VMEM((ns, GLA_HEADS, DVH, DKH), F32)],
        compiler_params=_cparams(("arbitrary", "arbitrary")),
        name="gla_core",
    )(*args)
    return res[:ns], res[ns]


def _gla_out_kernel(o_ref, r_ref, x_ref, mod_ref, gn_ref, w_ref, gpost_ref, x1_ref):
    o = o_ref[...].reshape(TILE, GLA_HEADS * DVH)
    r = r_ref[...].reshape(TILE, GLA_HEADS * DVH)
    gn = gn_ref[...]
    parts = [_rms(o[:, h * DVH:(h + 1) * DVH], gn) for h in range(GLA_HEADS)]
    y = jnp.concatenate(parts, axis=-1) * (r * _sigmoid(r))
    mix = _dot(y.astype(BF16), w_ref[...]).reshape(TILE_CHUNKS, CHUNK, D_MODEL)
    m = mod_ref[...]
    x1_ref[...] = x_ref[...] + m[:, 2:3, :] * _rms(mix, gpost_ref[...])


def _gla_out(o, r, x, modc, g_norm, w_out, g_post):
    nc = x.shape[0]
    tok = lambda w: pl.BlockSpec((TILE_CHUNKS, CHUNK, w), lambda i: (i, 0, 0))
    full = lambda a: pl.BlockSpec(a.shape, lambda i: (0,) * a.ndim)
    return pl.pallas_call(
        _gla_out_kernel,
        grid=(nc // TILE_CHUNKS,),
        in_specs=[tok(GLA_HEADS * DVH), tok(GLA_HEADS * DVH), tok(D_MODEL),
                  pl.BlockSpec((TILE_CHUNKS, 6, D_MODEL), lambda i: (i, 0, 0)),
                  full(g_norm), full(w_out), full(g_post)],
        out_specs=tok(D_MODEL),
        out_shape=jax.ShapeDtypeStruct(x.shape, F32),
        compiler_params=_cparams(("arbitrary",)),
        name="gla_out",
    )(o, r, x, modc, g_norm, w_out, g_post)


def _conv_kernel(x_ref, mod_ref, hist0_ref, gpre_ref, w1_ref, b1_ref, wdw_ref, bdw_ref, lng_ref, lnb_ref,
                 w2_ref, b2_ref, gpost_ref, x1_ref, hist_out_ref, ext_scr, shift_scr, *, nseq, ntiles):
    j = pl.program_id(1)
    tt = TILE // nseq

    @pl.when(j == 0)
    def _():
        for s in range(nseq):
            ext_scr[s, 0:HIST, :] = hist0_ref[s]

    m = mod_ref[...]
    h = _rms(x_ref[...], gpre_ref[...]) * (1.0 + m[:, 1:2, :]) + m[:, 0:1, :]
    p = _dot(h.reshape(TILE, D_MODEL).astype(BF16), w1_ref[...]) + b1_ref[...]
    u = p[:, :D_MODEL] * _sigmoid(p[:, D_MODEL:])
    first = HIST - (CONV_WIDTH - 1)
    ys = []
    span = tt + HIST - 8
    for s in range(nseq):
        ext_scr[s, HIST:HIST + tt, :] = u[s * tt:(s + 1) * tt]
        for d in range(1, 8):
            shift_scr[d - 1, :, :] = ext_scr[s, d:d + span, :]
        cols = []
        for c in range(D_MODEL // 128):
            lanes = slice(c * 128, (c + 1) * 128)
            acc = jnp.zeros((tt, 128), F32)
            for tap in range(CONV_WIDTH):
                off = first + tap
                if off % 8 == 0:
                    win = ext_scr[s, off:off + tt, lanes]
                else:
                    win = shift_scr[off % 8 - 1, off - off % 8:off - off % 8 + tt, lanes]
                acc = acc + win * wdw_ref[tap:tap + 1, lanes]
            cols.append(acc)
        ys.append(jnp.concatenate(cols, axis=-1))
        hist_new = ext_scr[s, tt:tt + HIST, :]
        ext_scr[s, 0:HIST, :] = hist_new

        @pl.when(j == ntiles - 1)
        def _():
            hist_out_ref[s] = hist_new
    y = jnp.concatenate(ys, axis=0) + bdw_ref[...]
    yc = y - jnp.mean(y, axis=-1, keepdims=True)
    yn = yc * lax.rsqrt(jnp.mean(yc * yc, axis=-1, keepdims=True) + EPS) * lng_ref[...] + lnb_ref[...]
    act = yn * _sigmoid(yn)
    mix = (_dot(act.astype(BF16), w2_ref[...]) + b2_ref[...]).reshape(TILE_CHUNKS, CHUNK, D_MODEL)
    x1_ref[...] = x_ref[...] + m[:, 2:3, :] * _rms(mix, gpost_ref[...])


def _conv(x, modc, hist0, params, *, chunk0, nseq_total, ntiles, nseq):
    tile0 = chunk0 // TILE_CHUNKS
    if ntiles == 1:
        cidx = lambda s, j: (tile0 + s, 0, 0)
        oidx = lambda s, j: (s, 0, 0)
    else:
        assert nseq == 1
        cidx = lambda s, j: (tile0 + s * ntiles + j, 0, 0)
        oidx = lambda s, j: (s * ntiles + j, 0, 0)
    full = lambda a: pl.BlockSpec(a.shape, lambda s, j: (0,) * a.ndim)
    hist_spec = pl.BlockSpec((nseq, HIST, D_MODEL), lambda s, j: (s, 0, 0))
    nblocks = nseq_total // nseq
    return pl.pallas_call(
        functools.partial(_conv_kernel, nseq=nseq, ntiles=ntiles),
        grid=(nblocks, ntiles),
        in_specs=[pl.BlockSpec((TILE_CHUNKS, CHUNK, D_MODEL), cidx),
                  pl.BlockSpec((TILE_CHUNKS, 6, D_MODEL), cidx),
                  hist_spec] + [full(a) for a in params],
        out_specs=[pl.BlockSpec((TILE_CHUNKS, CHUNK, D_MODEL), oidx), hist_spec],
        out_shape=[jax.ShapeDtypeStruct((nblocks * ntiles * TILE_CHUNKS, CHUNK, D_MODEL), F32),
                   jax.ShapeDtypeStruct((nseq_total, HIST, D_MODEL), F32)],
        scratch_shapes=[pltpu.VMEM((nseq, HIST + TILE // nseq, D_MODEL), F32),
                        pltpu.VMEM((7, TILE // nseq + HIST - 8, D_MODEL), F32)],
        compiler_params=_cparams(("arbitrary", "arbitrary")),
        name="conv",
    )(x, modc, hist0, *params)


def _moe_input(x_ref, mod_ref, g_ref):
    m = mod_ref[...]
    return _rms(x_ref[...], g_ref[...]) * (1.0 + m[:, 4:5, :]) + m[:, 3:4, :]


def _router_kernel(x_ref, mod_ref, g_ref, wrt_ref, br_ref,
                   gate_ref, gpos_ref, ce_ref, cbase_ref, nch_ref, cnt_ref, carry_scr):
    i = pl.program_id(0)

    @pl.when(i == 0)
    def _():
        carry_scr[...] = jnp.zeros_like(carry_scr)

    h2 = _moe_input(x_ref, mod_ref, g_ref).reshape(TILE, D_MODEL)
    h_hi = h2.astype(BF16)
    h_lo = (h2 - h_hi.astype(F32)).astype(BF16)
    w = wrt_ref[...]
    w_hi = w.astype(BF16)
    w_lo = (w - w_hi.astype(F32)).astype(BF16)
    logits = _dot_nt(w_hi, h_hi) + _dot_nt(w_lo, h_hi) + _dot_nt(w_hi, h_lo) + br_ref[...]
    e_iota = lax.broadcasted_iota(jnp.int32, (N_EXPERTS, TILE), 0)
    vals, sels = [], []
    lt = logits
    for _ in range(TOP_K):
        mx = jnp.max(lt, axis=0, keepdims=True)
        ix = jnp.min(jnp.where(lt == mx, e_iota, N_EXPERTS), axis=0, keepdims=True)
        sl = e_iota == ix
        vals.append(mx)
        sels.append(sl)
        lt = jnp.where(sl, -jnp.inf, lt)
    ex = [jnp.exp(v - vals[0]) for v in vals]
    den = ex[0] + ex[1] + ex[2] + ex[3]
    gate_ref[...] = _stack_rows([e / den for e in ex])
    chosen = (sels[0] | sels[1] | sels[2] | sels[3])
    chosen_f = jnp.where(chosen, 1.0, 0.0)
    rr = lax.broadcasted_iota(jnp.int32, (TILE, TILE), 0)
    cc = lax.broadcasted_iota(jnp.int32, (TILE, TILE), 1)
    before = (rr < cc).astype(BF16)
    local = _dot(chosen_f.astype(BF16), before)
    cnt = jnp.sum(chosen_f, axis=1, keepdims=True)
    padded = jnp.ceil(cnt * (1.0 / ROW_CHUNK)) * ROW_CHUNK
    er = lax.broadcasted_iota(jnp.int32, (N_EXPERTS, N_EXPERTS), 0)
    ec = lax.broadcasted_iota(jnp.int32, (N_EXPERTS, N_EXPERTS), 1)
    lower = (ec < er).astype(BF16)
    goff = _dot(lower, jnp.broadcast_to(padded, (N_EXPERTS, 128)).astype(BF16))[:, 0:1]
    gpos = [jnp.sum(jnp.where(sl, goff + local, 0.0), axis=0, keepdims=True) for sl in sels]
    gpos_ref[...] = _stack_rows(gpos).astype(jnp.int32)
    carry = carry_scr[...][:, 0:1]
    crow = lax.broadcasted_iota(jnp.int32, (N_EXPERTS, CHUNK_SLOTS), 1).astype(F32) * ROW_CHUNK
    ce = jnp.sum(jnp.where(goff + padded <= crow, 1.0, 0.0), axis=0, keepdims=True)
    ce = jnp.minimum(ce, N_EXPERTS - 1.0)
    e_f = lax.broadcasted_iota(jnp.int32, (N_EXPERTS, CHUNK_SLOTS), 0).astype(F32)
    cbase = jnp.sum(jnp.where(e_f == ce, carry - goff, 0.0), axis=0, keepdims=True) + crow[0:1]
    ce_ref[0] = ce.astype(jnp.int32)
    cbase_ref[0] = cbase.astype(jnp.int32)
    nch = jnp.sum(padded, axis=0, keepdims=True) * (1.0 / ROW_CHUNK)
    nch_ref[0] = jnp.broadcast_to(nch, (1, 128)).astype(jnp.int32)
    carry_scr[...] = carry_scr[...] + padded
    cnt_ref[...] = carry_scr[...]


def _router(x, modc, g_pre, w_router_t, b_router):
    nc = x.shape[0]
    nt = nc // TILE_CHUNKS
    n = nc * CHUNK
    full = lambda a: pl.BlockSpec(a.shape, lambda i: (0,) * a.ndim)
    tok = pl.BlockSpec((TILE_CHUNKS, CHUNK, D_MODEL), lambda i: (i, 0, 0))
    kt = pl.BlockSpec((TOP_K, TILE), lambda i: (0, i))
    tab = lambda w: pl.BlockSpec((1, 1, w), lambda i: (i, 0, 0))
    return pl.pallas_call(
        _router_kernel,
        grid=(nt,),
        in_specs=[tok, pl.BlockSpec((TILE_CHUNKS, 6, D_MODEL), lambda i: (i, 0, 0)),
                  full(g_pre), full(w_router_t), full(b_router)],
        out_specs=[kt, kt, tab(CHUNK_SLOTS), tab(CHUNK_SLOTS), tab(128),
                   pl.BlockSpec((N_EXPERTS, 128), lambda i: (0, 0))],
        out_shape=[jax.ShapeDtypeStruct((TOP_K, n), F32),
                   jax.ShapeDtypeStruct((TOP_K, n), jnp.int32),
                   jax.ShapeDtypeStruct((nt, 1, CHUNK_SLOTS), jnp.int32),
                   jax.ShapeDtypeStruct((nt, 1, CHUNK_SLOTS), jnp.int32),
                   jax.ShapeDtypeStruct((nt, 1, 128), jnp.int32),
                   jax.ShapeDtypeStruct((N_EXPERTS, 128), F32)],
        scratch_shapes=[pltpu.VMEM((N_EXPERTS, 128), F32)],
        compiler_params=_cparams(("arbitrary",)),
        name="router",
    )(x, modc, g_pre, w_router_t, b_router)


def _row_copy(src, dst, sem):
    return pltpu.make_async_copy(src, dst, sem)


def _dispatch_kernel(cdst_ref, nch_ref, pend_ref, x_ref, mod_ref, g_ref, gpos_ref, rows_ref,
                     grp_scr, zero_scr, sem, zsem):
    i = pl.program_id(0)
    nt = pl.num_programs(0)
    slot = i % 2

    @pl.when(i == 0)
    def _():
        zero_scr[...] = jnp.zeros_like(zero_scr)

        n_blocks = rows_ref.shape[0] // EXPERT_ROWS
        total = pend_ref[N_EXPERTS - 1]

        def block_to_clear(e):
            is_tail = e >= N_EXPERTS
            ee = jnp.minimum(e, N_EXPERTS - 1)
            end = pend_ref[ee]
            start = jnp.where(ee == 0, 0, pend_ref[jnp.maximum(ee - 1, 0)])
            tail_row = total + (e - N_EXPERTS) * EXPERT_ROWS
            row0 = jnp.where(is_tail, tail_row, jnp.maximum(end - EXPERT_ROWS, 0))
            needed = jnp.where(is_tail, tail_row < n_blocks * EXPERT_ROWS, end > start)
            row0 = jnp.minimum(row0, (n_blocks - 1) * EXPERT_ROWS)
            dst = rows_ref.at[pl.ds(pl.multiple_of(row0, EXPERT_ROWS), EXPERT_ROWS)]
            return needed, _row_copy(zero_scr, dst, zsem)

        def clear(e, c):
            needed, cp = block_to_clear(e)

            @pl.when(needed)
            def _():
                cp.start()
            return c

        def settle(e, c):
            needed, cp = block_to_clear(e)

            @pl.when(needed)
            def _():
                cp.wait()
            return c

        n_tail = n_blocks - (TILE * TOP_K * nt) // EXPERT_ROWS
        lax.fori_loop(0, N_EXPERTS + n_tail, clear, 0)
        lax.fori_loop(0, N_EXPERTS + n_tail, settle, 0)

    h = _moe_input(x_ref, mod_ref, g_ref).reshape(TILE, D_MODEL).astype(BF16)
    gp = gpos_ref[...]
    jr = lax.broadcasted_iota(jnp.int32, (GROUP_ROWS, TILE), 0)
    hit = jr == gp[0:1]
    for k in range(1, TOP_K):
        hit = hit | (jr == gp[k:k + 1])
    grp_scr[slot] = _dot(jnp.where(hit, 1.0, 0.0).astype(BF16), h)

    def chunk_copy(tile, sl, c):
        d = cdst_ref[tile * CHUNK_SLOTS + c]
        src = grp_scr.at[sl, pl.ds(pl.multiple_of(c * ROW_CHUNK, ROW_CHUNK), ROW_CHUNK)]
        return _row_copy(src, rows_ref.at[pl.ds(pl.multiple_of(d, ROW_CHUNK), ROW_CHUNK)], sem.at[sl])

    n_own = nch_ref[i]

    def issue(c2, carry):
        c = 2 * c2
        chunk_copy(i, slot, c).start(priority=0)

        @pl.when(c + 1 < n_own)
        def _():
            chunk_copy(i, slot, c + 1).start(priority=1)
        return carry

    lax.fori_loop(0, (n_own + 1) // 2, issue, 0)

    def drain(tile, sl):
        def body(c, carry):
            chunk_copy(tile, sl, c).wait()
            return carry
        lax.fori_loop(0, nch_ref[tile], body, 0)

    @pl.when(i > 0)
    def _():
        drain(i - 1, 1 - slot)

    @pl.when(i == nt - 1)
    def _():
        drain(i, slot)


def _dispatch(chunk_dst, n_chunks, p_ends, x, modc, g_pre, gpos_t, n_rows):
    nc = x.shape[0]
    idx3 = lambda i, cd, nch, pend: (i, 0, 0)
    return pl.pallas_call(
        _dispatch_kernel,
        grid_spec=pltpu.PrefetchScalarGridSpec(
            num_scalar_prefetch=3,
            grid=(nc // TILE_CHUNKS,),
            in_specs=[pl.BlockSpec((TILE_CHUNKS, CHUNK, D_MODEL), idx3),
                      pl.BlockSpec((TILE_CHUNKS, 6, D_MODEL), idx3),
                      pl.BlockSpec(g_pre.shape, lambda i, cd, nch, pend: (0, 0)),
                      pl.BlockSpec((TOP_K, TILE), lambda i, cd, nch, pend: (0, i))],
            out_specs=pl.BlockSpec(memory_space=pl.ANY),
            scratch_shapes=[pltpu.VMEM((2, GROUP_ROWS, D_MODEL), F32), pltpu.VMEM((EXPERT_ROWS, D_MODEL), F32),
                            pltpu.SemaphoreType.DMA((2,)), pltpu.SemaphoreType.DMA],
        ),
        out_shape=jax.ShapeDtypeStruct((n_rows, D_MODEL), F32),
        compiler_params=_cparams(("arbitrary",)),
        name="dispatch",
    )(chunk_dst, n_chunks, p_ends, x, modc, g_pre, gpos_t)


def _experts_kernel(be_ref, nu_ref, run_ref, nxt_ref, x_ref, b1_ref, b2_ref, w1_hbm, w2_hbm, y_ref,
                    w1_f32, w2_f32, w1_scr, w2_scr, sem, *, layer):
    i = pl.program_id(0)
    valid = i < nu_ref[0]
    e = be_ref[i]
    changed = jnp.logical_or(i == 0, e != be_ref[jnp.maximum(i - 1, 0)])
    slot = run_ref[i] % 2
    dff = w2_scr.shape[0]

    def fetch(expert, sl):
        return (_row_copy(w1_hbm.at[layer, expert], w1_f32.at[sl], sem.at[sl, 0]),
                _row_copy(w2_hbm.at[layer, expert], w2_f32.at[sl], sem.at[sl, 1]))

    @pl.when(i == 0)
    def _():
        for cp in fetch(e, 0):
            cp.start()

    @pl.when(jnp.logical_and(valid, changed))
    def _():
        nxt = nxt_ref[i]

        @pl.when(nxt >= 0)
        def _():
            for cp in fetch(nxt, 1 - slot):
                cp.start()

        for cp in fetch(e, slot):
            cp.wait()
        rows = 128

        def cast(c, carry):
            r0 = pl.multiple_of(c * rows, rows)
            w1_scr[pl.ds(r0, rows), :] = w1_f32[slot, pl.ds(r0, rows), :].astype(BF16)
            w2_scr[pl.ds(r0, rows), :] = w2_f32[slot, pl.ds(r0, rows), :].astype(BF16)
            return carry

        lax.fori_loop(0, D_MODEL // rows, cast, 0)

    @pl.when(valid)
    def _():
        h = _dot(x_ref[...].astype(BF16), w1_scr[...]) + b1_ref[0, 0]
        gate = jnp.minimum(h[:, :dff], SWIGLU_LIMIT)
        up = jnp.clip(h[:, dff:], -SWIGLU_LIMIT, SWIGLU_LIMIT)
        act = (up + 1.0) * (gate * _sigmoid(SWIGLU_ALPHA * gate))
        y_ref[...] = _dot(act.astype(BF16), w2_scr[...]) + b2_ref[0, 0]

    @pl.when(jnp.logical_not(valid))
    def _():
        y_ref[...] = jnp.zeros_like(y_ref)


def _experts(blk_e, n_used, blk_run, blk_next, x_rows, w1, b1, w2, b2, layer):
    n_rows = x_rows.shape[0]
    n_blocks = n_rows // EXPERT_ROWS
    nl, e, d, f2 = w1.shape
    dff = w2.shape[2]
    assert d == dff == D_MODEL
    row_idx = lambda i, be, nu, run, nxt: (jnp.minimum(i, nu[0] - 1), 0)
    b_idx = lambda i, be, nu, run, nxt: (layer, be[i], 0, 0)
    return pl.pallas_call(
        functools.partial(_experts_kernel, layer=layer),
        grid_spec=pltpu.PrefetchScalarGridSpec(
            num_scalar_prefetch=4,
            grid=(n_blocks,),
            in_specs=[pl.BlockSpec((EXPERT_ROWS, D_MODEL), row_idx),
                      pl.BlockSpec((1, 1, 1, f2), b_idx),
                      pl.BlockSpec((1, 1, 1, D_MODEL), b_idx),
                      pl.BlockSpec(memory_space=pl.ANY),
                      pl.BlockSpec(memory_space=pl.ANY)],
            out_specs=pl.BlockSpec((EXPERT_ROWS, D_MODEL), lambda i, be, nu, run, nxt: (i, 0)),
            scratch_shapes=[pltpu.VMEM((2, d, f2), F32), pltpu.VMEM((2, dff, D_MODEL), F32),
                            pltpu.VMEM((d, f2), BF16), pltpu.VMEM((dff, D_MODEL), BF16),
                            pltpu.SemaphoreType.DMA((2, 2))],
        ),
        out_shape=jax.ShapeDtypeStruct((n_rows, D_MODEL), F32),
        compiler_params=_cparams(("arbitrary",), EXPERTS_VMEM_LIMIT),
        name="experts",
    )(blk_e, n_used, blk_run, blk_next, x_rows, b1.reshape(nl, e, 1, f2), b2.reshape(nl, e, 1, D_MODEL), w1, w2)


def _combine_kernel(cdst_ref, nch_ref, gate_ref, gpos_ref, x_ref, mod_ref, gpost_ref, rows_ref, o_ref, ybuf, sem):
    i = pl.program_id(0)
    nt = pl.num_programs(0)
    slot = i % 2

    def chunk_copy(tile, sl, c):
        d = cdst_ref[tile * CHUNK_SLOTS + c]
        dst = ybuf.at[sl, pl.ds(pl.multiple_of(c * ROW_CHUNK, ROW_CHUNK), ROW_CHUNK)]
        return _row_copy(rows_ref.at[pl.ds(pl.multiple_of(d, ROW_CHUNK), ROW_CHUNK)], dst, sem.at[sl])

    def gather_tile(tile, sl):
        n_tile = nch_ref[tile]

        def issue(c2, carry):
            c = 2 * c2
            chunk_copy(tile, sl, c).start(priority=0)

            @pl.when(c + 1 < n_tile)
            def _():
                chunk_copy(tile, sl, c + 1).start(priority=1)
            return carry

        lax.fori_loop(0, (n_tile + 1) // 2, issue, 0)

    @pl.when(i == 0)
    def _():
        ybuf[...] = jnp.zeros_like(ybuf)
        gather_tile(0, 0)

    @pl.when(i + 1 < nt)
    def _():
        gather_tile(i + 1, 1 - slot)

    def drain(c, carry):
        chunk_copy(i, slot, c).wait()
        return carry

    lax.fori_loop(0, nch_ref[i], drain, 0)

    g = gate_ref[...]
    gp = gpos_ref[...]
    jc = lax.broadcasted_iota(jnp.int32, (TILE, GROUP_ROWS), 1)
    gm = jnp.where(jc == gp[:, 0:1], g[:, 0:1], 0.0)
    for k in range(1, TOP_K):
        gm = gm + jnp.where(jc == gp[:, k:k + 1], g[:, k:k + 1], 0.0)
    gm_hi = gm.astype(BF16)
    gm_lo = (gm - gm_hi.astype(F32)).astype(BF16)
    yb = ybuf[slot].astype(BF16)
    y = _dot(gm_hi, yb) + _dot(gm_lo, yb)
    m = mod_ref[...]
    y = y.reshape(TILE_CHUNKS, CHUNK, D_MODEL)
    o_ref[...] = x_ref[...] + m[:, 5:6, :] * _rms(y, gpost_ref[...])


def _combine(chunk_dst, n_chunks, gates, gpos, x, modc, g_post, y_rows):
    nc = x.shape[0]
    idx3 = lambda i, cd, nch: (i, 0, 0)
    tok = pl.BlockSpec((TILE_CHUNKS, CHUNK, D_MODEL), idx3)
    per_tok = pl.BlockSpec((TILE, TOP_K), lambda i, cd, nch: (i, 0))
    return pl.pallas_call(
        _combine_kernel,
        grid_spec=pltpu.PrefetchScalarGridSpec(
            num_scalar_prefetch=2,
            grid=(nc // TILE_CHUNKS,),
            in_specs=[per_tok, per_tok, tok,
                      pl.BlockSpec((TILE_CHUNKS, 6, D_MODEL), idx3),
                      pl.BlockSpec(g_post.shape, lambda i, cd, nch: (0, 0)),
                      pl.BlockSpec(memory_space=pl.ANY)],
            out_specs=tok,
            scratch_shapes=[pltpu.VMEM((2, GROUP_ROWS, D_MODEL), F32), pltpu.SemaphoreType.DMA((2,))],
        ),
        out_shape=jax.ShapeDtypeStruct(x.shape, F32),
        compiler_params=_cparams(("arbitrary",)),
        name="combine",
    )(chunk_dst, n_chunks, gates, gpos, x, modc, g_post, y_rows)


def _moe_layer(x, modc, g_pre, g_post, w_router, b_router, w1, b1, w2, b2, layer):
    nc = x.shape[0]
    n = nc * CHUNK
    nt = n // TILE
    gate_t, gpos_t, chunk_e, chunk_base, n_chunks, cnt = _router(
        x, modc, g_pre, w_router.T, b_router.reshape(N_EXPERTS, 1))
    counts = cnt[:, 0].astype(jnp.int32)
    padded = (counts + EXPERT_ROWS - 1) // EXPERT_ROWS * EXPERT_ROWS
    p_ends = jnp.cumsum(padded).astype(jnp.int32)
    p_starts = p_ends - padded
    max_rows = n * TOP_K + nt * N_EXPERTS * (ROW_CHUNK - 1)
    n_blocks = -(-max_rows // EXPERT_ROWS) + N_EXPERTS
    n_used = p_ends[-1] // EXPERT_ROWS
    blk = jnp.minimum(jnp.arange(n_blocks, dtype=jnp.int32), n_used - 1) * EXPERT_ROWS
    blk_e = jnp.minimum(jnp.sum((p_ends[None, :] <= blk[:, None]).astype(jnp.int32), axis=1), N_EXPERTS - 1)
    eids = jnp.arange(N_EXPERTS, dtype=jnp.int32)
    chunk_e = chunk_e.reshape(nt, CHUNK_SLOTS)
    chunk_dst = jnp.sum(jnp.where(chunk_e[:, :, None] == eids, p_starts, 0), axis=-1) + chunk_base.reshape(nt, CHUNK_SLOTS)
    chunk_dst = chunk_dst.reshape(-1)
    n_chunks = n_chunks[:, 0, 0]
    blk_run = jnp.cumsum(jnp.concatenate([jnp.ones((1,), jnp.int32),
                                          (blk_e[1:] != blk_e[:-1]).astype(jnp.int32)])) - 1
    later_used = (eids[None, :] > eids[:, None]) & (counts > 0)[None, :]
    next_used = jnp.min(jnp.where(later_used, eids[None, :], N_EXPERTS), axis=1)
    next_used = jnp.where(next_used == N_EXPERTS, -1, next_used)
    blk_next = jnp.sum(jnp.where(blk_e[:, None] == eids, next_used, 0), axis=1)
    x_rows = _dispatch(chunk_dst, n_chunks, p_ends, x, modc, g_pre, gpos_t, n_blocks * EXPERT_ROWS)
    y_rows = _experts(blk_e, n_used.reshape(1), blk_run, blk_next, x_rows, w1, b1, w2, b2, layer)
    return _combine(chunk_dst, n_chunks, gate_t.T, gpos_t.T, x, modc, g_post, y_rows)


def kernel(x_prompt, x_sample, state_gla, cache_conv, c_prompt, c_sample, w_ada, b_ada, norm_pre, norm_post, w_gla_in, w_gla_gate_up, b_gla_gate, g_gla_norm, w_gla_out, w_pw1, b_pw1, w_dw, b_dw, ln_g, ln_b, w_pw2, b_pw2, w_router, b_router, w_e1, b_e1, w_e2, b_e2):
    bp, lp, d = x_prompt.shape
    bs, ls, _ = x_sample.shape
    depth = w_ada.shape[0]
    assert d == D_MODEL and ls == CHUNK and lp % TILE == 0 and bs % TILE_CHUNKS == 0
    assert bp == GLA_SEQS and bs % GLA_SEQS == 0
    ncp = bp * lp // CHUNK
    ncs = bs
    nseq = bp + bs
    nch = lp // CHUNK

    x = jnp.concatenate([x_prompt.reshape(ncp, CHUNK, d), x_sample.reshape(ncs, CHUNK, d)], axis=0)
    rows = -(-nseq // 8) * 8
    c_all = jnp.concatenate([c_prompt, c_sample, jnp.zeros((rows - nseq, d), F32)], axis=0)
    mod = _ada(c_all, w_ada, b_ada)
    mod_p = jnp.broadcast_to(mod[:, :bp, None, :], (depth, bp, nch, 6 * d)).reshape(depth, ncp, 6 * d)
    modc = jnp.concatenate([mod_p, mod[:, bp:nseq]], axis=1).reshape(depth, ncp + ncs, 6, d)

    row = lambda a: a.reshape(1, -1)
    dk = GLA_HEADS * DKH
    dv = GLA_HEADS * DVH
    gla_states, conv_hists = [], []
    for i in range(depth):
        j = i // 2
        if i % 2 == 0:
            w_in = w_gla_in[j]
            rank = w_gla_gate_up.shape[1]
            w_main = w_in[:, :2 * dk + 2 * dv].astype(BF16)
            w_gz = jnp.pad(w_in[:, 2 * dk + 2 * dv:], ((0, 0), (0, 128 - rank))).astype(BF16)
            w_gu = jnp.pad(w_gla_gate_up[j], ((0, 128 - rank), (0, 0))).astype(BF16)
            q, k, v, r, lg = _gla_in(x, modc[i], row(norm_pre[i, 0]), w_main, w_gz, w_gu, row(b_gla_gate[j]))
            s0_p = jnp.zeros((bp, GLA_HEADS, DVH, DKH), F32)
            s0_s = jnp.swapaxes(state_gla[j], -1, -2)
            o_p, st_p = _gla_core(q, k, v, lg, s0_p, chunk0=0, nblocks=1, nchunks=nch)
            o_s, st_s = _gla_core(q, k, v, lg, s0_s, chunk0=ncp, nblocks=bs // GLA_SEQS, nchunks=1)
            o_s = jnp.stack(o_s, axis=1).reshape(ncs, CHUNK, dv)
            o = jnp.concatenate(list(o_p) + [o_s], axis=0)
            gla_states.append((jnp.swapaxes(st_p, -1, -2), jnp.swapaxes(st_s, -1, -2)))
            x = _gla_out(o, r, x, modc[i], row(g_gla_norm[j]), w_gla_out[j].astype(BF16), row(norm_post[i, 0]))
        else:
            params = (row(norm_pre[i, 0]), w_pw1[j].astype(BF16), row(b_pw1[j]), w_dw[j], row(b_dw[j]),
                      row(ln_g[j]), row(ln_b[j]), w_pw2[j].astype(BF16), row(b_pw2[j]), row(norm_post[i, 0]))
            keep = CONV_WIDTH - 1
            h0_p = jnp.zeros((bp, HIST, d), F32)
            h0_s = jnp.pad(cache_conv[j], ((0, 0), (HIST - keep, 0), (0, 0)))
            x_p, hp = _conv(x, modc[i], h0_p, params, chunk0=0, nseq_total=bp, ntiles=lp // TILE, nseq=1)
            x_s, hs = _conv(x, modc[i], h0_s, params, chunk0=ncp, nseq_total=bs, ntiles=1, nseq=TILE_CHUNKS)
            x = jnp.concatenate([x_p, x_s], axis=0)
            conv_hists.append((hp[:, HIST - keep:], hs[:, HIST - keep:]))
        x = _moe_layer(x, modc[i], row(norm_pre[i, 1]), row(norm_post[i, 1]),
                       w_router[i], b_router[i], w_e1, b_e1, w_e2, b_e2, i)

    y_prompt = x[:ncp].reshape(bp, lp, d)
    y_sample = x[ncp:].reshape(bs, ls, d)
    gla_p = jnp.stack([s[0] for s in gla_states])
    gla_s = jnp.stack([s[1] for s in gla_states])
    conv_p = jnp.stack([c[0] for c in conv_hists])
    conv_s = jnp.stack([c[1] for c in conv_hists])
    return (y_prompt, y_sample, gla_p, gla_s, conv_p, conv_s)
```

```python
import functools

import jax
import jax.numpy as jnp
from jax import lax
from jax.experimental import pallas as pl
from jax.experimental.pallas import tpu as pltpu

F32 = jnp.float32
BF16 = jnp.bfloat16

D_MODEL = 1024
CHUNK = 64
TILE_CHUNKS = 4
TILE = CHUNK * TILE_CHUNKS
GLA_HEADS = 4
DKH = 128
DVH = 256
SUB = 16
GLA_SEQS = 4
GATE_TAU = 16.0
CONV_WIDTH = 31
HIST = 32
N_EXPERTS = 32
TOP_K = 4
SWIGLU_LIMIT = 7.0
SWIGLU_ALPHA = 1.702
EXPERT_ROWS = 256
ROW_CHUNK = 8
GROUP_ROWS = TILE * TOP_K + N_EXPERTS * ROW_CHUNK
CHUNK_SLOTS = 256
WAIT_CHUNKS = 16
EPS = 1e-6
VMEM_LIMIT = 48 * 1024 * 1024
EXPERTS_VMEM_LIMIT = 56 * 1024 * 1024


def _cparams(sem, vmem=VMEM_LIMIT):
    return pltpu.CompilerParams(dimension_semantics=sem, vmem_limit_bytes=vmem)


def _dot(a, b):
    return jnp.dot(a, b, preferred_element_type=F32)


def _dot_nt(a, b):
    return lax.dot_general(a, b, (((1,), (1,)), ((), ())), preferred_element_type=F32)


def _dot_tn(a, b):
    return lax.dot_general(a, b, (((0,), (0,)), ((), ())), preferred_element_type=F32)


def _sigmoid(x):
    return 1.0 / (1.0 + jnp.exp(-x))


def _rms(x, g):
    return x * lax.rsqrt(jnp.mean(x * x, axis=-1, keepdims=True) + EPS) * g


def _stack_rows(rows):
    n = rows[0].shape[1]
    ri = lax.broadcasted_iota(jnp.int32, (len(rows), n), 0)
    out = jnp.broadcast_to(rows[0], (len(rows), n))
    for j in range(1, len(rows)):
        out = jnp.where(ri == j, jnp.broadcast_to(rows[j], (len(rows), n)), out)
    return out


def _split3(x):
    hi = x.astype(BF16)
    r1 = x - hi.astype(F32)
    mid = r1.astype(BF16)
    lo = (r1 - mid.astype(F32)).astype(BF16)
    return hi, mid, lo


def _ada_kernel(c_ref, w_ref, b_ref, o_ref):
    c = c_ref[...]
    cs = (c * _sigmoid(c)).astype(BF16)
    o_ref[0] = _dot(cs, w_ref[0].astype(BF16)) + b_ref[0]


def _ada(c_all, w_ada, b_ada):
    depth, d, n6 = w_ada.shape
    rows = c_all.shape[0]
    nt = n6 // d
    return pl.pallas_call(
        _ada_kernel,
        grid=(depth, nt),
        in_specs=[
            pl.BlockSpec((rows, d), lambda i, n: (0, 0)),
            pl.BlockSpec((1, d, d), lambda i, n: (i, 0, n)),
            pl.BlockSpec((1, 1, d), lambda i, n: (i, 0, n)),
        ],
        out_specs=pl.BlockSpec((1, rows, d), lambda i, n: (i, 0, n)),
        out_shape=jax.ShapeDtypeStruct((depth, rows, n6), F32),
        compiler_params=_cparams(("arbitrary", "arbitrary")),
        name="ada",
    )(c_all, w_ada, b_ada.reshape(depth, 1, n6))


def _gla_in_kernel(x_ref, mod_ref, g_ref, w_ref, wgz_ref, wgu_ref, bg_ref,
                   q_ref, k_ref, v_ref, r_ref, lg_ref):
    m = mod_ref[...]
    h = _rms(x_ref[...], g_ref[...]) * (1.0 + m[:, 1:2, :]) + m[:, 0:1, :]
    hb = h.reshape(TILE, D_MODEL).astype(BF16)
    p = _dot(hb, w_ref[...])
    dk = GLA_HEADS * DKH
    dv = GLA_HEADS * DVH
    q_ref[...] = (p[:, :dk] * (DKH ** -0.5)).reshape(TILE_CHUNKS, CHUNK, dk)
    k_ref[...] = p[:, dk:2 * dk].reshape(TILE_CHUNKS, CHUNK, dk)
    v_ref[...] = p[:, 2 * dk:2 * dk + dv].reshape(TILE_CHUNKS, CHUNK, dv)
    r_ref[...] = p[:, 2 * dk + dv:].reshape(TILE_CHUNKS, CHUNK, dv)
    gz = _dot(hb, wgz_ref[...])
    z = _dot(gz.astype(BF16), wgu_ref[...]) + bg_ref[...]
    ls = jnp.minimum(z, 0.0) - jnp.log(1.0 + jnp.exp(-jnp.abs(z)))
    lg_ref[...] = (ls / GATE_TAU).reshape(TILE_CHUNKS, CHUNK, dk)


def _gla_in(x, modc, g_pre, w_main, w_gz, w_gu, b_gate):
    nc = x.shape[0]
    dk = GLA_HEADS * DKH
    dv = GLA_HEADS * DVH
    tok = lambda w: pl.BlockSpec((TILE_CHUNKS, CHUNK, w), lambda i: (i, 0, 0))
    full = lambda a: pl.BlockSpec(a.shape, lambda i: (0,) * a.ndim)
    outs = [jax.ShapeDtypeStruct((nc, CHUNK, w), F32) for w in (dk, dk, dv, dv, dk)]
    return pl.pallas_call(
        _gla_in_kernel,
        grid=(nc // TILE_CHUNKS,),
        in_specs=[tok(D_MODEL), pl.BlockSpec((TILE_CHUNKS, 6, D_MODEL), lambda i: (i, 0, 0)),
                  full(g_pre), full(w_main), full(w_gz), full(w_gu), full(b_gate)],
        out_specs=[tok(dk), tok(dk), tok(dv), tok(dv), tok(dk)],
        out_shape=outs,
        compiler_params=_cparams(("arbitrary",)),
        name="gla_in",
    )(x, modc, g_pre, w_main, w_gz, w_gu, b_gate)


def _gla_core_kernel(*refs, nchunks):
    ns = GLA_SEQS
    q_refs, k_refs, v_refs, lg_refs = (refs[g * ns:(g + 1) * ns] for g in range(4))
    s0_ref = refs[4 * ns]
    o_refs = refs[4 * ns + 1:5 * ns + 1]
    sout_ref, s_scr = refs[5 * ns + 1], refs[5 * ns + 2]
    j = pl.program_id(1)

    @pl.when(j == 0)
    def _():
        s_scr[...] = s0_ref[...]

    nsub = CHUNK // SUB
    npair = SUB * SUB
    rr = lax.broadcasted_iota(jnp.int32, (CHUNK, CHUNK), 0)
    cc = lax.broadcasted_iota(jnp.int32, (CHUNK, CHUNK), 1)
    tril = (cc <= rr).astype(BF16)
    ones_k = jnp.ones((DKH, DKH), BF16)
    pr = lax.broadcasted_iota(jnp.int32, (nsub * npair, 2 * CHUNK), 0)
    pc = lax.broadcasted_iota(jnp.int32, (nsub * npair, 2 * CHUNK), 1)
    ps, pt, pi = pr % SUB, (pr // SUB) % SUB, pr // npair
    place = ((pc == SUB * pi + ps) & (ps <= pt)).astype(F32)
    st = lax.broadcasted_iota(jnp.int32, (CHUNK, nsub * npair), 0)
    sp = lax.broadcasted_iota(jnp.int32, (CHUNK, nsub * npair), 1)
    sel = (sp // SUB == st).astype(BF16)
    row_sub = lax.broadcasted_iota(jnp.int32, (CHUNK, DKH), 0) // SUB
    key_row = lax.broadcasted_iota(jnp.int32, (CHUNK, DKH), 0)
    zeros_v = jnp.zeros((CHUNK, DVH), BF16)

    for b in range(ns):
        lg = lg_refs[b][0]
        hi, mid, lo = _split3(lg)
        bcum = _dot(tril, hi) + _dot(tril, mid) + _dot(tril, lo)
        q_all = q_refs[b][0]
        k_all = k_refs[b][0]
        v_all = v_refs[b][0]
        heads = range(GLA_HEADS)
        qs = [q_all[:, h * DKH:(h + 1) * DKH] for h in heads]
        ks = [k_all[:, h * DKH:(h + 1) * DKH] for h in heads]
        bs_ = [bcum[:, h * DKH:(h + 1) * DKH] for h in heads]
        vbs = [v_all[:, h * DVH:(h + 1) * DVH].astype(BF16) for h in heads]
        outs = []
        for h in heads:
            blast = bs_[h][CHUNK - 1:CHUNK, :]
            s_t = s_scr[b, h]
            outs.append(_dot_nt((qs[h] * jnp.exp(bs_[h])).astype(BF16), s_t.astype(BF16)))
            kdec = (ks[h] * jnp.exp(blast - bs_[h])).astype(BF16)
            s_scr[b, h] = s_t * jnp.exp(blast) + _dot_tn(vbs[h], kdec)
        pairs = []
        for h in heads:
            qh, kh, bh = qs[h], ks[h], bs_[h]
            pieces = []
            for t in range(CHUNK):
                nk = SUB // 2 if t % SUB < SUB // 2 else SUB
                k0 = t // SUB * SUB
                qt = jnp.broadcast_to(qh[t:t + 1], (nk, DKH))
                bq = jnp.broadcast_to(bh[t:t + 1], (nk, DKH))
                piece = qt * kh[k0:k0 + nk] * jnp.exp(jnp.minimum(bq - bh[k0:k0 + nk], 0.0))
                if nk < SUB:
                    piece = jnp.concatenate([piece, jnp.zeros((SUB - nk, DKH), F32)], axis=0)
                pieces.append(piece)
            pairs.append(jnp.concatenate(pieces, axis=0).astype(BF16))
        rsums = [_dot(pairs[h], ones_k) for h in heads]
        a_diag = [_dot(sel, (rsums[h] * place).astype(BF16)) for h in heads]
        a_off = []
        for h in heads:
            qh, kh, bh = qs[h], ks[h], bs_[h]
            qd, kd = [], []
            for i in range(1, nsub):
                anc = bh[SUB * i - 1:SUB * i, :]
                qd.append(jnp.where(row_sub == i, qh * jnp.exp(jnp.minimum(bh - anc, 0.0)), 0.0))
                kd.append(jnp.where(key_row < SUB * i, kh * jnp.exp(jnp.minimum(anc - bh, 0.0)), 0.0))
            qd = jnp.concatenate(qd, axis=-1).astype(BF16)
            kd = jnp.concatenate(kd, axis=-1).astype(BF16)
            a_off.append(_dot_nt(qd, jnp.concatenate([kd, jnp.zeros_like(kd)], axis=0)))
        for h in heads:
            a = (a_diag[h] + a_off[h]).astype(BF16)
            o = outs[h] + _dot(a, jnp.concatenate([vbs[h], zeros_v], axis=0))
            o_refs[b][0, :, h * DVH:(h + 1) * DVH] = o

    @pl.when(j == nchunks - 1)
    def _():
        sout_ref[...] = s_scr[...]


def _gla_core(q, k, v, lg, s0_t, *, chunk0, nblocks, nchunks):
    dk = GLA_HEADS * DKH
    dv = GLA_HEADS * DVH
    ns = GLA_SEQS

    def tok(w, g):
        return pl.BlockSpec((1, CHUNK, w), lambda s, j: (chunk0 + (s * ns + g) * nchunks + j, 0, 0))

    st_spec = pl.BlockSpec((ns, GLA_HEADS, DVH, DKH), lambda s, j: (s, 0, 0, 0))
    o_spec = pl.BlockSpec((1, CHUNK, dv), lambda s, j: (s * nchunks + j, 0, 0))
    in_specs = [tok(w, g) for w in (dk, dk, dv, dk) for g in range(ns)] + [st_spec]
    args = [a for a in (q, k, v, lg) for _ in range(ns)] + [s0_t]
    res = pl.pallas_call(
        functools.partial(_gla_core_kernel, nchunks=nchunks),
        grid=(nblocks, nchunks),
        in_specs=in_specs,
        out_specs=[o_spec] * ns + [st_spec],
        out_shape=[jax.ShapeDtypeStruct((nblocks * nchunks, CHUNK, dv), F32)] * ns
        + [jax.ShapeDtypeStruct((nblocks * ns, GLA_HEADS, DVH, DKH), F32)],
        scratch_shapes=[pltpu.VMEM((ns, GLA_HEADS, DVH, DKH), F32)],
        compiler_params=_cparams(("arbitrary", "arbitrary")),
        name="gla_core",
    )(*args)
    return res[:ns], res[ns]


def _gla_out_kernel(o_ref, r_ref, x_ref, mod_ref, gn_ref, w_ref, gpost_ref, x1_ref):
    o = o_ref[...].reshape(TILE, GLA_HEADS * DVH)
    r = r_ref[...].reshape(TILE, GLA_HEADS * DVH)
    gn = gn_ref[...]
    parts = [_rms(o[:, h * DVH:(h + 1) * DVH], gn) for h in range(GLA_HEADS)]
    y = jnp.concatenate(parts, axis=-1) * (r * _sigmoid(r))
    mix = _dot(y.astype(BF16), w_ref[...]).reshape(TILE_CHUNKS, CHUNK, D_MODEL)
    m = mod_ref[...]
    x1_ref[...] = x_ref[...] + m[:, 2:3, :] * _rms(mix, gpost_ref[...])


def _gla_out(o, r, x, modc, g_norm, w_out, g_post):
    nc = x.shape[0]
    tok = lambda w: pl.BlockSpec((TILE_CHUNKS, CHUNK, w), lambda i: (i, 0, 0))
    full = lambda a: pl.BlockSpec(a.shape, lambda i: (0,) * a.ndim)
    return pl.pallas_call(
        _gla_out_kernel,
        grid=(nc // TILE_CHUNKS,),
        in_specs=[tok(GLA_HEADS * DVH), tok(GLA_HEADS * DVH), tok(D_MODEL),
                  pl.BlockSpec((TILE_CHUNKS, 6, D_MODEL), lambda i: (i, 0, 0)),
                  full(g_norm), full(w_out), full(g_post)],
        out_specs=tok(D_MODEL),
        out_shape=jax.ShapeDtypeStruct(x.shape, F32),
        compiler_params=_cparams(("arbitrary",)),
        name="gla_out",
    )(o, r, x, modc, g_norm, w_out, g_post)


def _conv_kernel(x_ref, mod_ref, hist0_ref, gpre_ref, w1_ref, b1_ref, wdw_ref, bdw_ref, lng_ref, lnb_ref,
                 w2_ref, b2_ref, gpost_ref, x1_ref, hist_out_ref, ext_scr, shift_scr, *, nseq, ntiles):
    j = pl.program_id(1)
    tt = TILE // nseq

    @pl.when(j == 0)
    def _():
        for s in range(nseq):
            ext_scr[s, 0:HIST, :] = hist0_ref[s]

    m = mod_ref[...]
    h = _rms(x_ref[...], gpre_ref[...]) * (1.0 + m[:, 1:2, :]) + m[:, 0:1, :]
    p = _dot(h.reshape(TILE, D_MODEL).astype(BF16), w1_ref[...]) + b1_ref[...]
    u = p[:, :D_MODEL] * _sigmoid(p[:, D_MODEL:])
    first = HIST - (CONV_WIDTH - 1)
    ys = []
    span = tt + HIST - 8
    for s in range(nseq):
        ext_scr[s, HIST:HIST + tt, :] = u[s * tt:(s + 1) * tt]
        for d in range(1, 8):
            shift_scr[d - 1, :, :] = ext_scr[s, d:d + span, :]
        cols = []
        for c in range(D_MODEL // 128):
            lanes = slice(c * 128, (c + 1) * 128)
            acc = jnp.zeros((tt, 128), F32)
            for tap in range(CONV_WIDTH):
                off = first + tap
                if off % 8 == 0:
                    win = ext_scr[s, off:off + tt, lanes]
                else:
                    win = shift_scr[off % 8 - 1, off - off % 8:off - off % 8 + tt, lanes]
                acc = acc + win * wdw_ref[tap:tap + 1, lanes]
            cols.append(acc)
        ys.append(jnp.concatenate(cols, axis=-1))
        hist_new = ext_scr[s, tt:tt + HIST, :]
        ext_scr[s, 0:HIST, :] = hist_new

        @pl.when(j == ntiles - 1)
        def _():
            hist_out_ref[s] = hist_new
    y = jnp.concatenate(ys, axis=0) + bdw_ref[...]
    yc = y - jnp.mean(y, axis=-1, keepdims=True)
    yn = yc * lax.rsqrt(jnp.mean(yc * yc, axis=-1, keepdims=True) + EPS) * lng_ref[...] + lnb_ref[...]
    act = yn * _sigmoid(yn)
    mix = (_dot(act.astype(BF16), w2_ref[...]) + b2_ref[...]).reshape(TILE_CHUNKS, CHUNK, D_MODEL)
    x1_ref[...] = x_ref[...] + m[:, 2:3, :] * _rms(mix, gpost_ref[...])


def _conv(x, modc, hist0, params, *, chunk0, nseq_total, ntiles, nseq):
    tile0 = chunk0 // TILE_CHUNKS
    if ntiles == 1:
        cidx = lambda s, j: (tile0 + s, 0, 0)
        oidx = lambda s, j: (s, 0, 0)
    else:
        assert nseq == 1
        cidx = lambda s, j: (tile0 + s * ntiles + j, 0, 0)
        oidx = lambda s, j: (s * ntiles + j, 0, 0)
    full = lambda a: pl.BlockSpec(a.shape, lambda s, j: (0,) * a.ndim)
    hist_spec = pl.BlockSpec((nseq, HIST, D_MODEL), lambda s, j: (s, 0, 0))
    nblocks = nseq_total // nseq
    return pl.pallas_call(
        functools.partial(_conv_kernel, nseq=nseq, ntiles=ntiles),
        grid=(nblocks, ntiles),
        in_specs=[pl.BlockSpec((TILE_CHUNKS, CHUNK, D_MODEL), cidx),
                  pl.BlockSpec((TILE_CHUNKS, 6, D_MODEL), cidx),
                  hist_spec] + [full(a) for a in params],
        out_specs=[pl.BlockSpec((TILE_CHUNKS, CHUNK, D_MODEL), oidx), hist_spec],
        out_shape=[jax.ShapeDtypeStruct((nblocks * ntiles * TILE_CHUNKS, CHUNK, D_MODEL), F32),
                   jax.ShapeDtypeStruct((nseq_total, HIST, D_MODEL), F32)],
        scratch_shapes=[pltpu.VMEM((nseq, HIST + TILE // nseq, D_MODEL), F32),
                        pltpu.VMEM((7, TILE // nseq + HIST - 8, D_MODEL), F32)],
        compiler_params=_cparams(("arbitrary", "arbitrary")),
        name="conv",
    )(x, modc, hist0, *params)


def _moe_input(x_ref, mod_ref, g_ref):
    m = mod_ref[...]
    return _rms(x_ref[...], g_ref[...]) * (1.0 + m[:, 4:5, :]) + m[:, 3:4, :]


def _router_kernel(x_ref, mod_ref, g_ref, wrt_ref, br_ref,
                   gate_ref, gpos_ref, ce_ref, cbase_ref, nch_ref, cnt_ref, carry_scr):
    i = pl.program_id(0)

    @pl.when(i == 0)
    def _():
        carry_scr[...] = jnp.zeros_like(carry_scr)

    h2 = _moe_input(x_ref, mod_ref, g_ref).reshape(TILE, D_MODEL)
    h_hi = h2.astype(BF16)
    h_lo = (h2 - h_hi.astype(F32)).astype(BF16)
    w = wrt_ref[...]
    w_hi = w.astype(BF16)
    w_lo = (w - w_hi.astype(F32)).astype(BF16)
    logits = _dot_nt(w_hi, h_hi) + _dot_nt(w_lo, h_hi) + _dot_nt(w_hi, h_lo) + br_ref[...]
    e_iota = lax.broadcasted_iota(jnp.int32, (N_EXPERTS, TILE), 0)
    vals, sels = [], []
    lt = logits
    for _ in range(TOP_K):
        mx = jnp.max(lt, axis=0, keepdims=True)
        ix = jnp.min(jnp.where(lt == mx, e_iota, N_EXPERTS), axis=0, keepdims=True)
        sl = e_iota == ix
        vals.append(mx)
        sels.append(sl)
        lt = jnp.where(sl, -jnp.inf, lt)
    ex = [jnp.exp(v - vals[0]) for v in vals]
    den = ex[0] + ex[1] + ex[2] + ex[3]
    gate_ref[...] = _stack_rows([e / den for e in ex])
    chosen = (sels[0] | sels[1] | sels[2] | sels[3])
    chosen_f = jnp.where(chosen, 1.0, 0.0)
    rr = lax.broadcasted_iota(jnp.int32, (TILE, TILE), 0)
    cc = lax.broadcasted_iota(jnp.int32, (TILE, TILE), 1)
    before = (rr < cc).astype(BF16)
    local = _dot(chosen_f.astype(BF16), before)
    cnt = jnp.sum(chosen_f, axis=1, keepdims=True)
    padded = jnp.ceil(cnt * (1.0 / ROW_CHUNK)) * ROW_CHUNK
    er = lax.broadcasted_iota(jnp.int32, (N_EXPERTS, N_EXPERTS), 0)
    ec = lax.broadcasted_iota(jnp.int32, (N_EXPERTS, N_EXPERTS), 1)
    lower = (ec < er).astype(BF16)
    goff = _dot(lower, jnp.broadcast_to(padded, (N_EXPERTS, 128)).astype(BF16))[:, 0:1]
    gpos = [jnp.sum(jnp.where(sl, goff + local, 0.0), axis=0, keepdims=True) for sl in sels]
    gpos_ref[...] = _stack_rows(gpos).astype(jnp.int32)
    carry = carry_scr[...][:, 0:1]
    crow = lax.broadcasted_iota(jnp.int32, (N_EXPERTS, CHUNK_SLOTS), 1).astype(F32) * ROW_CHUNK
    ce = jnp.sum(jnp.where(goff + padded <= crow, 1.0, 0.0), axis=0, keepdims=True)
    ce = jnp.minimum(ce, N_EXPERTS - 1.0)
    e_f = lax.broadcasted_iota(jnp.int32, (N_EXPERTS, CHUNK_SLOTS), 0).astype(F32)
    cbase = jnp.sum(jnp.where(e_f == ce, carry - goff, 0.0), axis=0, keepdims=True) + crow[0:1]
    ce_ref[0] = ce.astype(jnp.int32)
    cbase_ref[0] = cbase.astype(jnp.int32)
    nch = jnp.sum(padded, axis=0, keepdims=True) * (1.0 / ROW_CHUNK)
    nch_ref[0] = jnp.broadcast_to(nch, (1, 128)).astype(jnp.int32)
    carry_scr[...] = carry_scr[...] + padded
    cnt_ref[...] = carry_scr[...]


def _router(x, modc, g_pre, w_router_t, b_router):
    nc = x.shape[0]
    nt = nc // TILE_CHUNKS
    n = nc * CHUNK
    full = lambda a: pl.BlockSpec(a.shape, lambda i: (0,) * a.ndim)
    tok = pl.BlockSpec((TILE_CHUNKS, CHUNK, D_MODEL), lambda i: (i, 0, 0))
    kt = pl.BlockSpec((TOP_K, TILE), lambda i: (0, i))
    tab = lambda w: pl.BlockSpec((1, 1, w), lambda i: (i, 0, 0))
    return pl.pallas_call(
        _router_kernel,
        grid=(nt,),
        in_specs=[tok, pl.BlockSpec((TILE_CHUNKS, 6, D_MODEL), lambda i: (i, 0, 0)),
                  full(g_pre), full(w_router_t), full(b_router)],
        out_specs=[kt, kt, tab(CHUNK_SLOTS), tab(CHUNK_SLOTS), tab(128),
                   pl.BlockSpec((N_EXPERTS, 128), lambda i: (0, 0))],
        out_shape=[jax.ShapeDtypeStruct((TOP_K, n), F32),
                   jax.ShapeDtypeStruct((TOP_K, n), jnp.int32),
                   jax.ShapeDtypeStruct((nt, 1, CHUNK_SLOTS), jnp.int32),
                   jax.ShapeDtypeStruct((nt, 1, CHUNK_SLOTS), jnp.int32),
                   jax.ShapeDtypeStruct((nt, 1, 128), jnp.int32),
                   jax.ShapeDtypeStruct((N_EXPERTS, 128), F32)],
        scratch_shapes=[pltpu.VMEM((N_EXPERTS, 128), F32)],
        compiler_params=_cparams(("arbitrary",)),
        name="router",
    )(x, modc, g_pre, w_router_t, b_router)


def _row_copy(src, dst, sem):
    return pltpu.make_async_copy(src, dst, sem)


def _dispatch_kernel(cdst_ref, nch_ref, pend_ref, x_ref, mod_ref, g_ref, gpos_ref, rows_ref,
                     grp_scr, zero_scr, sem, zsem):
    i = pl.program_id(0)
    nt = pl.num_programs(0)
    slot = i % 2

    @pl.when(i == 0)
    def _():
        zero_scr[...] = jnp.zeros_like(zero_scr)

        n_blocks = rows_ref.shape[0] // EXPERT_ROWS
        total = pend_ref[N_EXPERTS - 1]

        def block_to_clear(e):
            is_tail = e >= N_EXPERTS
            ee = jnp.minimum(e, N_EXPERTS - 1)
            end = pend_ref[ee]
            start = jnp.where(ee == 0, 0, pend_ref[jnp.maximum(ee - 1, 0)])
            tail_row = total + (e - N_EXPERTS) * EXPERT_ROWS
            row0 = jnp.where(is_tail, tail_row, jnp.maximum(end - EXPERT_ROWS, 0))
            needed = jnp.where(is_tail, tail_row < n_blocks * EXPERT_ROWS, end > start)
            row0 = jnp.minimum(row0, (n_blocks - 1) * EXPERT_ROWS)
            dst = rows_ref.at[pl.ds(pl.multiple_of(row0, EXPERT_ROWS), EXPERT_ROWS)]
            return needed, _row_copy(zero_scr, dst, zsem)

        def clear(e, c):
            needed, cp = block_to_clear(e)

            @pl.when(needed)
            def _():
                cp.start()
            return c

        def settle(e, c):
            needed, cp = block_to_clear(e)

            @pl.when(needed)
            def _():
                cp.wait()
            return c

        n_tail = n_blocks - (TILE * TOP_K * nt) // EXPERT_ROWS
        lax.fori_loop(0, N_EXPERTS + n_tail, clear, 0)
        lax.fori_loop(0, N_EXPERTS + n_tail, settle, 0)

    h = _moe_input(x_ref, mod_ref, g_ref).reshape(TILE, D_MODEL).astype(BF16)
    gp = gpos_ref[...]
    jr = lax.broadcasted_iota(jnp.int32, (GROUP_ROWS, TILE), 0)
    hit = jr == gp[0:1]
    for k in range(1, TOP_K):
        hit = hit | (jr == gp[k:k + 1])
    grp_scr[slot] = _dot(jnp.where(hit, 1.0, 0.0).astype(BF16), h)

    def chunk_copy(tile, sl, c):
        d = cdst_ref[tile * CHUNK_SLOTS + c]
        src = grp_scr.at[sl, pl.ds(pl.multiple_of(c * ROW_CHUNK, ROW_CHUNK), ROW_CHUNK)]
        return _row_copy(src, rows_ref.at[pl.ds(pl.multiple_of(d, ROW_CHUNK), ROW_CHUNK)], sem.at[sl])

    n_own = nch_ref[i]

    def issue(c2, carry):
        c = 2 * c2
        chunk_copy(i, slot, c).start(priority=0)

        @pl.when(c + 1 < n_own)
        def _():
            chunk_copy(i, slot, c + 1).start(priority=1)
        return carry

    lax.fori_loop(0, (n_own + 1) // 2, issue, 0)

    def drain(tile, sl):
        n_all = nch_ref[tile]
        many = WAIT_CHUNKS * ROW_CHUNK
        bulk = _row_copy(grp_scr.at[sl, pl.ds(0, many)], rows_ref.at[pl.ds(0, many)], sem.at[sl])

        def bulk_wait(c, carry):
            bulk.wait()
            return carry

        def single_wait(c, carry):
            chunk_copy(tile, sl, 0).wait()
            return carry

        lax.fori_loop(0, n_all // WAIT_CHUNKS, bulk_wait, 0)
        lax.fori_loop(0, n_all % WAIT_CHUNKS, single_wait, 0)

    @pl.when(i > 0)
    def _():
        drain(i - 1, 1 - slot)

    @pl.when(i == nt - 1)
    def _():
        drain(i, slot)


def _dispatch(chunk_dst, n_chunks, p_ends, x, modc, g_pre, gpos_t, n_rows):
    nc = x.shape[0]
    idx3 = lambda i, cd, nch, pend: (i, 0, 0)
    return pl.pallas_call(
        _dispatch_kernel,
        grid_spec=pltpu.PrefetchScalarGridSpec(
            num_scalar_prefetch=3,
            grid=(nc // TILE_CHUNKS,),
            in_specs=[pl.BlockSpec((TILE_CHUNKS, CHUNK, D_MODEL), idx3),
                      pl.BlockSpec((TILE_CHUNKS, 6, D_MODEL), idx3),
                      pl.BlockSpec(g_pre.shape, lambda i, cd, nch, pend: (0, 0)),
                      pl.BlockSpec((TOP_K, TILE), lambda i, cd, nch, pend: (0, i))],
            out_specs=pl.BlockSpec(memory_space=pl.ANY),
            scratch_shapes=[pltpu.VMEM((2, GROUP_ROWS, D_MODEL), F32), pltpu.VMEM((EXPERT_ROWS, D_MODEL), F32),
                            pltpu.SemaphoreType.DMA((2,)), pltpu.SemaphoreType.DMA],
        ),
        out_shape=jax.ShapeDtypeStruct((n_rows, D_MODEL), F32),
        compiler_params=_cparams(("arbitrary",)),
        name="dispatch",
    )(chunk_dst, n_chunks, p_ends, x, modc, g_pre, gpos_t)


def _experts_kernel(be_ref, nu_ref, run_ref, nxt_ref, x_ref, b1_ref, b2_ref, w1_hbm, w2_hbm, y_ref,
                    w1_f32, w2_f32, w1_scr, w2_scr, sem, *, layer):
    i = pl.program_id(0)
    valid = i < nu_ref[0]
    e = be_ref[i]
    changed = jnp.logical_or(i == 0, e != be_ref[jnp.maximum(i - 1, 0)])
    slot = run_ref[i] % 2
    dff = w2_scr.shape[0]

    def fetch(expert, sl):
        return (_row_copy(w1_hbm.at[layer, expert], w1_f32.at[sl], sem.at[sl, 0]),
                _row_copy(w2_hbm.at[layer, expert], w2_f32.at[sl], sem.at[sl, 1]))

    @pl.when(i == 0)
    def _():
        for cp in fetch(e, 0):
            cp.start()

    @pl.when(jnp.logical_and(valid, changed))
    def _():
        nxt = nxt_ref[i]

        @pl.when(nxt >= 0)
        def _():
            for cp in fetch(nxt, 1 - slot):
                cp.start()

        for cp in fetch(e, slot):
            cp.wait()
        rows = 128

        def cast(c, carry):
            r0 = pl.multiple_of(c * rows, rows)
            w1_scr[pl.ds(r0, rows), :] = w1_f32[slot, pl.ds(r0, rows), :].astype(BF16)
            w2_scr[pl.ds(r0, rows), :] = w2_f32[slot, pl.ds(r0, rows), :].astype(BF16)
            return carry

        lax.fori_loop(0, D_MODEL // rows, cast, 0)

    @pl.when(valid)
    def _():
        h = _dot(x_ref[...].astype(BF16), w1_scr[...]) + b1_ref[0, 0]
        gate = jnp.minimum(h[:, :dff], SWIGLU_LIMIT)
        up = jnp.clip(h[:, dff:], -SWIGLU_LIMIT, SWIGLU_LIMIT)
        act = (up + 1.0) * (gate * _sigmoid(SWIGLU_ALPHA * gate))
        y_ref[...] = _dot(act.astype(BF16), w2_scr[...]) + b2_ref[0, 0]

    @pl.when(jnp.logical_not(valid))
    def _():
        y_ref[...] = jnp.zeros_like(y_ref)


def _experts(blk_e, n_used, blk_run, blk_next, x_rows, w1, b1, w2, b2, layer):
    n_rows = x_rows.shape[0]
    n_blocks = n_rows // EXPERT_ROWS
    nl, e, d, f2 = w1.shape
    dff = w2.shape[2]
    assert d == dff == D_MODEL
    row_idx = lambda i, be, nu, run, nxt: (jnp.minimum(i, nu[0] - 1), 0)
    b_idx = lambda i, be, nu, run, nxt: (layer, be[i], 0, 0)
    return pl.pallas_call(
        functools.partial(_experts_kernel, layer=layer),
        grid_spec=pltpu.PrefetchScalarGridSpec(
            num_scalar_prefetch=4,
            grid=(n_blocks,),
            in_specs=[pl.BlockSpec((EXPERT_ROWS, D_MODEL), row_idx),
                      pl.BlockSpec((1, 1, 1, f2), b_idx),
                      pl.BlockSpec((1, 1, 1, D_MODEL), b_idx),
                      pl.BlockSpec(memory_space=pl.ANY),
                      pl.BlockSpec(memory_space=pl.ANY)],
            out_specs=pl.BlockSpec((EXPERT_ROWS, D_MODEL), lambda i, be, nu, run, nxt: (i, 0)),
            scratch_shapes=[pltpu.VMEM((2, d, f2), F32), pltpu.VMEM((2, dff, D_MODEL), F32),
                            pltpu.VMEM((d, f2), BF16), pltpu.VMEM((dff, D_MODEL), BF16),
                            pltpu.SemaphoreType.DMA((2, 2))],
        ),
        out_shape=jax.ShapeDtypeStruct((n_rows, D_MODEL), F32),
        compiler_params=_cparams(("arbitrary",), EXPERTS_VMEM_LIMIT),
        name="experts",
    )(blk_e, n_used, blk_run, blk_next, x_rows, b1.reshape(nl, e, 1, f2), b2.reshape(nl, e, 1, D_MODEL), w1, w2)


def _combine_kernel(cdst_ref, nch_ref, gate_ref, gpos_ref, x_ref, mod_ref, gpost_ref, rows_ref, o_ref, ybuf, sem):
    i = pl.program_id(0)
    nt = pl.num_programs(0)
    slot = i % 2

    def chunk_copy(tile, sl, c):
        d = cdst_ref[tile * CHUNK_SLOTS + c]
        dst = ybuf.at[sl, pl.ds(pl.multiple_of(c * ROW_CHUNK, ROW_CHUNK), ROW_CHUNK)]
        return _row_copy(rows_ref.at[pl.ds(pl.multiple_of(d, ROW_CHUNK), ROW_CHUNK)], dst, sem.at[sl])

    def gather_tile(tile, sl):
        n_tile = nch_ref[tile]

        def issue(c2, carry):
            c = 2 * c2
            chunk_copy(tile, sl, c).start(priority=0)

            @pl.when(c + 1 < n_tile)
            def _():
                chunk_copy(tile, sl, c + 1).start(priority=1)
            return carry

        lax.fori_loop(0, (n_tile + 1) // 2, issue, 0)

    @pl.when(i == 0)
    def _():
        ybuf[...] = jnp.zeros_like(ybuf)
        gather_tile(0, 0)

    @pl.when(i + 1 < nt)
    def _():
        gather_tile(i + 1, 1 - slot)

    n_own = nch_ref[i]
    many = WAIT_CHUNKS * ROW_CHUNK
    bulk = _row_copy(rows_ref.at[pl.ds(0, many)], ybuf.at[slot, pl.ds(0, many)], sem.at[slot])

    def bulk_wait(c, carry):
        bulk.wait()
        return carry

    def single_wait(c, carry):
        chunk_copy(i, slot, 0).wait()
        return carry

    lax.fori_loop(0, n_own // WAIT_CHUNKS, bulk_wait, 0)
    lax.fori_loop(0, n_own % WAIT_CHUNKS, single_wait, 0)

    g = gate_ref[...]
    gp = gpos_ref[...]
    jc = lax.broadcasted_iota(jnp.int32, (TILE, GROUP_ROWS), 1)
    gm = jnp.where(jc == gp[:, 0:1], g[:, 0:1], 0.0)
    for k in range(1, TOP_K):
        gm = gm + jnp.where(jc == gp[:, k:k + 1], g[:, k:k + 1], 0.0)
    gm_hi = gm.astype(BF16)
    gm_lo = (gm - gm_hi.astype(F32)).astype(BF16)
    yb = ybuf[slot].astype(BF16)
    y = _dot(gm_hi, yb) + _dot(gm_lo, yb)
    m = mod_ref[...]
    y = y.reshape(TILE_CHUNKS, CHUNK, D_MODEL)
    o_ref[...] = x_ref[...] + m[:, 5:6, :] * _rms(y, gpost_ref[...])


def _combine(chunk_dst, n_chunks, gates, gpos, x, modc, g_post, y_rows):
    nc = x.shape[0]
    idx3 = lambda i, cd, nch: (i, 0, 0)
    tok = pl.BlockSpec((TILE_CHUNKS, CHUNK, D_MODEL), idx3)
    per_tok = pl.BlockSpec((TILE, TOP_K), lambda i, cd, nch: (i, 0))
    return pl.pallas_call(
        _combine_kernel,
        grid_spec=pltpu.PrefetchScalarGridSpec(
            num_scalar_prefetch=2,
            grid=(nc // TILE_CHUNKS,),
            in_specs=[per_tok, per_tok, tok,
                      pl.BlockSpec((TILE_CHUNKS, 6, D_MODEL), idx3),
                      pl.BlockSpec(g_post.shape, lambda i, cd, nch: (0, 0)),
                      pl.BlockSpec(memory_space=pl.ANY)],
            out_specs=tok,
            scratch_shapes=[pltpu.VMEM((2, GROUP_ROWS, D_MODEL), F32), pltpu.SemaphoreType.DMA((2,))],
        ),
        out_shape=jax.ShapeDtypeStruct(x.shape, F32),
        compiler_params=_cparams(("arbitrary",)),
        name="combine",
    )(chunk_dst, n_chunks, gates, gpos, x, modc, g_post, y_rows)


def _moe_layer(x, modc, g_pre, g_post, w_router, b_router, w1, b1, w2, b2, layer):
    nc = x.shape[0]
    n = nc * CHUNK
    nt = n // TILE
    gate_t, gpos_t, chunk_e, chunk_base, n_chunks, cnt = _router(
        x, modc, g_pre, w_router.T, b_router.reshape(N_EXPERTS, 1))
    counts = cnt[:, 0].astype(jnp.int32)
    padded = (counts + EXPERT_ROWS - 1) // EXPERT_ROWS * EXPERT_ROWS
    p_ends = jnp.cumsum(padded).astype(jnp.int32)
    p_starts = p_ends - padded
    max_rows = n * TOP_K + nt * N_EXPERTS * (ROW_CHUNK - 1)
    n_blocks = -(-max_rows // EXPERT_ROWS) + N_EXPERTS
    n_used = p_ends[-1] // EXPERT_ROWS
    blk = jnp.minimum(jnp.arange(n_blocks, dtype=jnp.int32), n_used - 1) * EXPERT_ROWS
    blk_e = jnp.minimum(jnp.sum((p_ends[None, :] <= blk[:, None]).astype(jnp.int32), axis=1), N_EXPERTS - 1)
    eids = jnp.arange(N_EXPERTS, dtype=jnp.int32)
    chunk_e = chunk_e.reshape(nt, CHUNK_SLOTS)
    chunk_dst = jnp.sum(jnp.where(chunk_e[:, :, None] == eids, p_starts, 0), axis=-1) + chunk_base.reshape(nt, CHUNK_SLOTS)
    chunk_dst = chunk_dst.reshape(-1)
    n_chunks = n_chunks[:, 0, 0]
    blk_run = jnp.cumsum(jnp.concatenate([jnp.ones((1,), jnp.int32),
                                          (blk_e[1:] != blk_e[:-1]).astype(jnp.int32)])) - 1
    later_used = (eids[None, :] > eids[:, None]) & (counts > 0)[None, :]
    next_used = jnp.min(jnp.where(later_used, eids[None, :], N_EXPERTS), axis=1)
    next_used = jnp.where(next_used == N_EXPERTS, -1, next_used)
    blk_next = jnp.sum(jnp.where(blk_e[:, None] == eids, next_used, 0), axis=1)
    x_rows = _dispatch(chunk_dst, n_chunks, p_ends, x, modc, g_pre, gpos_t, n_blocks * EXPERT_ROWS)
    y_rows = _experts(blk_e, n_used.reshape(1), blk_run, blk_next, x_rows, w1, b1, w2, b2, layer)
    return _combine(chunk_dst, n_chunks, gate_t.T, gpos_t.T, x, modc, g_post, y_rows)


def kernel(x_prompt, x_sample, state_gla, cache_conv, c_prompt, c_sample, w_ada, b_ada, norm_pre, norm_post, w_gla_in, w_gla_gate_up, b_gla_gate, g_gla_norm, w_gla_out, w_pw1, b_pw1, w_dw, b_dw, ln_g, ln_b, w_pw2, b_pw2, w_router, b_router, w_e1, b_e1, w_e2, b_e2):
    bp, lp, d = x_prompt.shape
    bs, ls, _ = x_sample.shape
    depth = w_ada.shape[0]
    assert d == D_MODEL and ls == CHUNK and lp % TILE == 0 and bs % TILE_CHUNKS == 0
    assert bp == GLA_SEQS and bs % GLA_SEQS == 0
    ncp = bp * lp // CHUNK
    ncs = bs
    nseq = bp + bs
    nch = lp // CHUNK

    x = jnp.concatenate([x_prompt.reshape(ncp, CHUNK, d), x_sample.reshape(ncs, CHUNK, d)], axis=0)
    rows = -(-nseq // 8) * 8
    c_all = jnp.concatenate([c_prompt, c_sample, jnp.zeros((rows - nseq, d), F32)], axis=0)
    mod = _ada(c_all, w_ada, b_ada)
    mod_p = jnp.broadcast_to(mod[:, :bp, None, :], (depth, bp, nch, 6 * d)).reshape(depth, ncp, 6 * d)
    modc = jnp.concatenate([mod_p, mod[:, bp:nseq]], axis=1).reshape(depth, ncp + ncs, 6, d)

    row = lambda a: a.reshape(1, -1)
    dk = GLA_HEADS * DKH
    dv = GLA_HEADS * DVH
    gla_states, conv_hists = [], []
    for i in range(depth):
        j = i // 2
        if i % 2 == 0:
            w_in = w_gla_in[j]
            rank = w_gla_gate_up.shape[1]
            w_main = w_in[:, :2 * dk + 2 * dv].astype(BF16)
            w_gz = jnp.pad(w_in[:, 2 * dk + 2 * dv:], ((0, 0), (0, 128 - rank))).astype(BF16)
            w_gu = jnp.pad(w_gla_gate_up[j], ((0, 128 - rank), (0, 0))).astype(BF16)
            q, k, v, r, lg = _gla_in(x, modc[i], row(norm_pre[i, 0]), w_main, w_gz, w_gu, row(b_gla_gate[j]))
            s0_p = jnp.zeros((bp, GLA_HEADS, DVH, DKH), F32)
            s0_s = jnp.swapaxes(state_gla[j], -1, -2)
            o_p, st_p = _gla_core(q, k, v, lg, s0_p, chunk0=0, nblocks=1, nchunks=nch)
            o_s, st_s = _gla_core(q, k, v, lg, s0_s, chunk0=ncp, nblocks=bs // GLA_SEQS, nchunks=1)
            o_s = jnp.stack(o_s, axis=1).reshape(ncs, CHUNK, dv)
            o = jnp.concatenate(list(o_p) + [o_s], axis=0)
            gla_states.append((jnp.swapaxes(st_p, -1, -2), jnp.swapaxes(st_s, -1, -2)))
            x = _gla_out(o, r, x, modc[i], row(g_gla_norm[j]), w_gla_out[j].astype(BF16), row(norm_post[i, 0]))
        else:
            params = (row(norm_pre[i, 0]), w_pw1[j].astype(BF16), row(b_pw1[j]), w_dw[j], row(b_dw[j]),
                      row(ln_g[j]), row(ln_b[j]), w_pw2[j].astype(BF16), row(b_pw2[j]), row(norm_post[i, 0]))
            keep = CONV_WIDTH - 1
            h0_p = jnp.zeros((bp, HIST, d), F32)
            h0_s = jnp.pad(cache_conv[j], ((0, 0), (HIST - keep, 0), (0, 0)))
            x_p, hp = _conv(x, modc[i], h0_p, params, chunk0=0, nseq_total=bp, ntiles=lp // TILE, nseq=1)
            x_s, hs = _conv(x, modc[i], h0_s, params, chunk0=ncp, nseq_total=bs, ntiles=1, nseq=TILE_CHUNKS)
            x = jnp.concatenate([x_p, x_s], axis=0)
            conv_hists.append((hp[:, HIST - keep:], hs[:, HIST - keep:]))
        x = _moe_layer(x, modc[i], row(norm_pre[i, 1]), row(norm_post[i, 1]),
                       w_router[i], b_router[i], w_e1, b_e1, w_e2, b_e2, i)

    y_prompt = x[:ncp].reshape(bp, lp, d)
    y_sample = x[ncp:].reshape(bs, ls, d)
    gla_p = jnp.stack([s[0] for s in gla_states])
    gla_s = jnp.stack([s[1] for s in gla_states])
    conv_p = jnp.stack([c[0] for c in conv_hists])
    conv_s = jnp.stack([c[1] for c in conv_hists])
    return (y_prompt, y_sample, gla_p, gla_s, conv_p, conv_s)
```

```python
import functools

import jax
import jax.numpy as jnp
from jax import lax
from jax.experimental import pallas as pl
from jax.experimental.pallas import tpu as pltpu

F32 = jnp.float32
BF16 = jnp.bfloat16

D_MODEL = 1024
CHUNK = 64
TILE_CHUNKS = 4
TILE = CHUNK * TILE_CHUNKS
GLA_HEADS = 4
DKH = 128
DVH = 256
SUB = 16
GLA_SEQS = 4
GATE_TAU = 16.0
CONV_WIDTH = 31
HIST = 32
N_EXPERTS = 32
TOP_K = 4
SWIGLU_LIMIT = 7.0
SWIGLU_ALPHA = 1.702
EXPERT_ROWS = 256
ROW_CHUNK = 8
GROUP_ROWS = TILE * TOP_K + N_EXPERTS * ROW_CHUNK
CHUNK_SLOTS = 256
WAIT_CHUNKS = 16
EPS = 1e-6
VMEM_LIMIT = 48 * 1024 * 1024
EXPERTS_VMEM_LIMIT = 56 * 1024 * 1024


def _cparams(sem, vmem=VMEM_LIMIT):
    return pltpu.CompilerParams(dimension_semantics=sem, vmem_limit_bytes=vmem)


def _dot(a, b):
    return jnp.dot(a, b, preferred_element_type=F32)


def _dot_nt(a, b):
    return lax.dot_general(a, b, (((1,), (1,)), ((), ())), preferred_element_type=F32)


def _dot_tn(a, b):
    return lax.dot_general(a, b, (((0,), (0,)), ((), ())), preferred_element_type=F32)


def _sigmoid(x):
    return 1.0 / (1.0 + jnp.exp(-x))


def _rms(x, g):
    return x * lax.rsqrt(jnp.mean(x * x, axis=-1, keepdims=True) + EPS) * g


def _stack_rows(rows):
    n = rows[0].shape[1]
    ri = lax.broadcasted_iota(jnp.int32, (len(rows), n), 0)
    out = jnp.broadcast_to(rows[0], (len(rows), n))
    for j in range(1, len(rows)):
        out = jnp.where(ri == j, jnp.broadcast_to(rows[j], (len(rows), n)), out)
    return out


def _split3(x):
    hi = x.astype(BF16)
    r1 = x - hi.astype(F32)
    mid = r1.astype(BF16)
    lo = (r1 - mid.astype(F32)).astype(BF16)
    return hi, mid, lo


def _two_part_specs(ntp, make):
    return (make(lambda i, *_: (jnp.minimum(i, ntp - 1), 0, 0)),
            make(lambda i, *_: (jnp.maximum(i - ntp, 0), 0, 0)))


def _pick_part(i, ntp, first_ref, second_ref):
    return jnp.where(i < ntp, first_ref[...], second_ref[...])


def _ada_kernel(c_ref, w_ref, b_ref, o_ref):
    c = c_ref[...]
    cs = (c * _sigmoid(c)).astype(BF16)
    o_ref[0] = _dot(cs, w_ref[0].astype(BF16)) + b_ref[0]


def _ada(c_all, w_ada, b_ada):
    depth, d, n6 = w_ada.shape
    rows = c_all.shape[0]
    nt = n6 // d
    return pl.pallas_call(
        _ada_kernel,
        grid=(depth, nt),
        in_specs=[
            pl.BlockSpec((rows, d), lambda i, n: (0, 0)),
            pl.BlockSpec((1, d, d), lambda i, n: (i, 0, n)),
            pl.BlockSpec((1, 1, d), lambda i, n: (i, 0, n)),
        ],
        out_specs=pl.BlockSpec((1, rows, d), lambda i, n: (i, 0, n)),
        out_shape=jax.ShapeDtypeStruct((depth, rows, n6), F32),
        compiler_params=_cparams(("arbitrary", "arbitrary")),
        name="ada",
    )(c_all, w_ada, b_ada.reshape(depth, 1, n6))


def _gla_in_kernel(xp_ref, xs_ref, mod_ref, g_ref, w_ref, wgz_ref, wgu_ref, bg_ref,
                   q_ref, k_ref, v_ref, r_ref, lg_ref, *, ntp):
    m = mod_ref[...]
    x = _pick_part(pl.program_id(0), ntp, xp_ref, xs_ref)
    h = _rms(x, g_ref[...]) * (1.0 + m[:, 1:2, :]) + m[:, 0:1, :]
    hb = h.reshape(TILE, D_MODEL).astype(BF16)
    p = _dot(hb, w_ref[...])
    dk = GLA_HEADS * DKH
    dv = GLA_HEADS * DVH
    q_ref[...] = (p[:, :dk] * (DKH ** -0.5)).reshape(TILE_CHUNKS, CHUNK, dk)
    k_ref[...] = p[:, dk:2 * dk].reshape(TILE_CHUNKS, CHUNK, dk)
    v_ref[...] = p[:, 2 * dk:2 * dk + dv].reshape(TILE_CHUNKS, CHUNK, dv)
    r_ref[...] = p[:, 2 * dk + dv:].reshape(TILE_CHUNKS, CHUNK, dv)
    gz = _dot(hb, wgz_ref[...])
    z = _dot(gz.astype(BF16), wgu_ref[...]) + bg_ref[...]
    ls = jnp.minimum(z, 0.0) - jnp.log(1.0 + jnp.exp(-jnp.abs(z)))
    lg_ref[...] = (ls / GATE_TAU).reshape(TILE_CHUNKS, CHUNK, dk)


def _gla_in(x_p, x_s, modc, g_pre, w_main, w_gz, w_gu, b_gate):
    nc = x_p.shape[0] + x_s.shape[0]
    ntp = x_p.shape[0] // TILE_CHUNKS
    dk = GLA_HEADS * DKH
    dv = GLA_HEADS * DVH
    tok = lambda w: pl.BlockSpec((TILE_CHUNKS, CHUNK, w), lambda i: (i, 0, 0))
    full = lambda a: pl.BlockSpec(a.shape, lambda i: (0,) * a.ndim)
    outs = [jax.ShapeDtypeStruct((nc, CHUNK, w), F32) for w in (dk, dk, dv, dv, dk)]
    xspecs = _two_part_specs(ntp, lambda f: pl.BlockSpec((TILE_CHUNKS, CHUNK, D_MODEL), f))
    return pl.pallas_call(
        functools.partial(_gla_in_kernel, ntp=ntp),
        grid=(nc // TILE_CHUNKS,),
        in_specs=[*xspecs, pl.BlockSpec((TILE_CHUNKS, 6, D_MODEL), lambda i: (i, 0, 0)),
                  full(g_pre), full(w_main), full(w_gz), full(w_gu), full(b_gate)],
        out_specs=[tok(dk), tok(dk), tok(dv), tok(dv), tok(dk)],
        out_shape=outs,
        compiler_params=_cparams(("arbitrary",)),
        name="gla_in",
    )(x_p, x_s, modc, g_pre, w_main, w_gz, w_gu, b_gate)


def _gla_core_kernel(*refs, nchunks):
    ns = GLA_SEQS
    q_refs, k_refs, v_refs, lg_refs = (refs[g * ns:(g + 1) * ns] for g in range(4))
    s0_ref = refs[4 * ns]
    o_refs = refs[4 * ns + 1:5 * ns + 1]
    sout_ref, s_scr = refs[5 * ns + 1], refs[5 * ns + 2]
    j = pl.program_id(1)

    @pl.when(j == 0)
    def _():
        s_scr[...] = s0_ref[...]

    nsub = CHUNK // SUB
    npair = SUB * SUB
    rr = lax.broadcasted_iota(jnp.int32, (CHUNK, CHUNK), 0)
    cc = lax.broadcasted_iota(jnp.int32, (CHUNK, CHUNK), 1)
    tril = (cc <= rr).astype(BF16)
    ones_k = jnp.ones((DKH, DKH), BF16)
    pr = lax.broadcasted_iota(jnp.int32, (nsub * npair, 2 * CHUNK), 0)
    pc = lax.broadcasted_iota(jnp.int32, (nsub * npair, 2 * CHUNK), 1)
    ps, pt, pi = pr % SUB, (pr // SUB) % SUB, pr // npair
    place = ((pc == SUB * pi + ps) & (ps <= pt)).astype(F32)
    st = lax.broadcasted_iota(jnp.int32, (CHUNK, nsub * npair), 0)
    sp = lax.broadcasted_iota(jnp.int32, (CHUNK, nsub * npair), 1)
    sel = (sp // SUB == st).astype(BF16)
    row_sub = lax.broadcasted_iota(jnp.int32, (CHUNK, DKH), 0) // SUB
    key_row = lax.broadcasted_iota(jnp.int32, (CHUNK, DKH), 0)
    zeros_v = jnp.zeros((CHUNK, DVH), BF16)

    for b in range(ns):
        lg = lg_refs[b][0]
        hi, mid, lo = _split3(lg)
        bcum = _dot(tril, hi) + _dot(tril, mid) + _dot(tril, lo)
        q_all = q_refs[b][0]
        k_all = k_refs[b][0]
        v_all = v_refs[b][0]
        heads = range(GLA_HEADS)
        qs = [q_all[:, h * DKH:(h + 1) * DKH] for h in heads]
        ks = [k_all[:, h * DKH:(h + 1) * DKH] for h in heads]
        bs_ = [bcum[:, h * DKH:(h + 1) * DKH] for h in heads]
        vbs = [v_all[:, h * DVH:(h + 1) * DVH].astype(BF16) for h in heads]
        outs = []
        for h in heads:
            blast = bs_[h][CHUNK - 1:CHUNK, :]
            s_t = s_scr[b, h]
            outs.append(_dot_nt((qs[h] * jnp.exp(bs_[h])).astype(BF16), s_t.astype(BF16)))
            kdec = (ks[h] * jnp.exp(blast - bs_[h])).astype(BF16)
            s_scr[b, h] = s_t * jnp.exp(blast) + _dot_tn(vbs[h], kdec)
        pairs = []
        for h in heads:
            qh, kh, bh = qs[h], ks[h], bs_[h]
            pieces = []
            for t in range(CHUNK):
                nk = SUB // 2 if t % SUB < SUB // 2 else SUB
                k0 = t // SUB * SUB
                qt = jnp.broadcast_to(qh[t:t + 1], (nk, DKH))
                bq = jnp.broadcast_to(bh[t:t + 1], (nk, DKH))
                piece = qt * kh[k0:k0 + nk] * jnp.exp(jnp.minimum(bq - bh[k0:k0 + nk], 0.0))
                if nk < SUB:
                    piece = jnp.concatenate([piece, jnp.zeros((SUB - nk, DKH), F32)], axis=0)
                pieces.append(piece)
            pairs.append(jnp.concatenate(pieces, axis=0).astype(BF16))
        rsums = [_dot(pairs[h], ones_k) for h in heads]
        a_diag = [_dot(sel, (rsums[h] * place).astype(BF16)) for h in heads]
        a_off = []
        for h in heads:
            qh, kh, bh = qs[h], ks[h], bs_[h]
            qd, kd = [], []
            for i in range(1, nsub):
                anc = bh[SUB * i - 1:SUB * i, :]
                qd.append(jnp.where(row_sub == i, qh * jnp.exp(jnp.minimum(bh - anc, 0.0)), 0.0))
                kd.append(jnp.where(key_row < SUB * i, kh * jnp.exp(jnp.minimum(anc - bh, 0.0)), 0.0))
            qd = jnp.concatenate(qd, axis=-1).astype(BF16)
            kd = jnp.concatenate(kd, axis=-1).astype(BF16)
            a_off.append(_dot_nt(qd, jnp.concatenate([kd, jnp.zeros_like(kd)], axis=0)))
        for h in heads:
            a = (a_diag[h] + a_off[h]).astype(BF16)
            o = outs[h] + _dot(a, jnp.concatenate([vbs[h], zeros_v], axis=0))
            o_refs[b][0, :, h * DVH:(h + 1) * DVH] = o

    @pl.when(j == nchunks - 1)
    def _():
        sout_ref[...] = s_scr[...]


def _gla_core(q, k, v, lg, s0_t, *, chunk0, nblocks, nchunks):
    dk = GLA_HEADS * DKH
    dv = GLA_HEADS * DVH
    ns = GLA_SEQS

    def tok(w, g):
        return pl.BlockSpec((1, CHUNK, w), lambda s, j: (chunk0 + (s * ns + g) * nchunks + j, 0, 0))

    st_spec = pl.BlockSpec((ns, GLA_HEADS, DVH, DKH), lambda s, j: (s, 0, 0, 0))
    o_spec = pl.BlockSpec((1, CHUNK, dv), lambda s, j: (s * nchunks + j, 0, 0))
    in_specs = [tok(w, g) for w in (dk, dk, dv, dk) for g in range(ns)] + [st_spec]
    args = [a for a in (q, k, v, lg) for _ in range(ns)] + [s0_t]
    res = pl.pallas_call(
        functools.partial(_gla_core_kernel, nchunks=nchunks),
        grid=(nblocks, nchunks),
        in_specs=in_specs,
        out_specs=[o_spec] * ns + [st_spec],
        out_shape=[jax.ShapeDtypeStruct((nblocks * nchunks, CHUNK, dv), F32)] * ns
        + [jax.ShapeDtypeStruct((nblocks * ns, GLA_HEADS, DVH, DKH), F32)],
        scratch_shapes=[pltpu.VMEM((ns, GLA_HEADS, DVH, DKH), F32)],
        compiler_params=_cparams(("arbitrary", "arbitrary")),
        name="gla_core",
    )(*args)
    return res[:ns], res[ns]


def _gla_out_kernel(*refs, ntp, tiles_per_seq):
    op_refs, os_refs = refs[:GLA_SEQS], refs[GLA_SEQS:2 * GLA_SEQS]
    r_ref, xp_ref, xs_ref, mod_ref, gn_ref, w_ref, gpost_ref, x1_ref = refs[2 * GLA_SEQS:]
    i = pl.program_id(0)
    o = jnp.concatenate([ref[...] for ref in os_refs], axis=0)
    for g in reversed(range(GLA_SEQS)):
        o = jnp.where(i < (g + 1) * tiles_per_seq, op_refs[g][...], o)
    o = o.reshape(TILE, GLA_HEADS * DVH)
    r = r_ref[...].reshape(TILE, GLA_HEADS * DVH)
    gn = gn_ref[...]
    parts = [_rms(o[:, h * DVH:(h + 1) * DVH], gn) for h in range(GLA_HEADS)]
    y = jnp.concatenate(parts, axis=-1) * (r * _sigmoid(r))
    mix = _dot(y.astype(BF16), w_ref[...]).reshape(TILE_CHUNKS, CHUNK, D_MODEL)
    m = mod_ref[...]
    x = _pick_part(pl.program_id(0), ntp, xp_ref, xs_ref)
    x1_ref[...] = x + m[:, 2:3, :] * _rms(mix, gpost_ref[...])


def _gla_out(o_p, o_s, r, x_p, x_s, modc, g_norm, w_out, g_post):
    nc = x_p.shape[0] + x_s.shape[0]
    ntp = x_p.shape[0] // TILE_CHUNKS
    assert len(o_p) == len(o_s) == GLA_SEQS == TILE_CHUNKS
    tps = ntp // GLA_SEQS
    dv = GLA_HEADS * DVH
    op_specs = [pl.BlockSpec((TILE_CHUNKS, CHUNK, dv), lambda i, g=g: (jnp.clip(i - g * tps, 0, tps - 1), 0, 0))
                for g in range(GLA_SEQS)]
    os_specs = [pl.BlockSpec((1, CHUNK, dv), lambda i: (jnp.maximum(i - ntp, 0), 0, 0))] * GLA_SEQS
    tok = lambda w: pl.BlockSpec((TILE_CHUNKS, CHUNK, w), lambda i: (i, 0, 0))
    full = lambda a: pl.BlockSpec(a.shape, lambda i: (0,) * a.ndim)
    xspecs = _two_part_specs(ntp, lambda f: pl.BlockSpec((TILE_CHUNKS, CHUNK, D_MODEL), f))
    return pl.pallas_call(
        functools.partial(_gla_out_kernel, ntp=ntp, tiles_per_seq=tps),
        grid=(nc // TILE_CHUNKS,),
        in_specs=[*op_specs, *os_specs, tok(GLA_HEADS * DVH), *xspecs,
                  pl.BlockSpec((TILE_CHUNKS, 6, D_MODEL), lambda i: (i, 0, 0)),
                  full(g_norm), full(w_out), full(g_post)],
        out_specs=tok(D_MODEL),
        out_shape=jax.ShapeDtypeStruct((nc, CHUNK, D_MODEL), F32),
        compiler_params=_cparams(("arbitrary",)),
        name="gla_out",
    )(*o_p, *o_s, r, x_p, x_s, modc, g_norm, w_out, g_post)


def _conv_kernel(x_ref, mod_ref, hist0_ref, gpre_ref, w1_ref, b1_ref, wdw_ref, bdw_ref, lng_ref, lnb_ref,
                 w2_ref, b2_ref, gpost_ref, x1_ref, hist_out_ref, ext_scr, shift_scr, *, nseq, ntiles):
    j = pl.program_id(1)
    tt = TILE // nseq

    @pl.when(j == 0)
    def _():
        for s in range(nseq):
            ext_scr[s, 0:HIST, :] = hist0_ref[s]

    m = mod_ref[...]
    h = _rms(x_ref[...], gpre_ref[...]) * (1.0 + m[:, 1:2, :]) + m[:, 0:1, :]
    p = _dot(h.reshape(TILE, D_MODEL).astype(BF16), w1_ref[...]) + b1_ref[...]
    u = p[:, :D_MODEL] * _sigmoid(p[:, D_MODEL:])
    first = HIST - (CONV_WIDTH - 1)
    ys = []
    span = tt + HIST - 8
    for s in range(nseq):
        ext_scr[s, HIST:HIST + tt, :] = u[s * tt:(s + 1) * tt]
        for d in range(1, 8):
            shift_scr[d - 1, :, :] = ext_scr[s, d:d + span, :]
        cols = []
        for c in range(D_MODEL // 128):
            lanes = slice(c * 128, (c + 1) * 128)
            acc = jnp.zeros((tt, 128), F32)
            for tap in range(CONV_WIDTH):
                off = first + tap
                if off % 8 == 0:
                    win = ext_scr[s, off:off + tt, lanes]
                else:
                    win = shift_scr[off % 8 - 1, off - off % 8:off - off % 8 + tt, lanes]
                acc = acc + win * wdw_ref[tap:tap + 1, lanes]
            cols.append(acc)
        ys.append(jnp.concatenate(cols, axis=-1))
        hist_new = ext_scr[s, tt:tt + HIST, :]
        ext_scr[s, 0:HIST, :] = hist_new

        @pl.when(j == ntiles - 1)
        def _():
            hist_out_ref[s] = hist_new
    y = jnp.concatenate(ys, axis=0) + bdw_ref[...]
    yc = y - jnp.mean(y, axis=-1, keepdims=True)
    yn = yc * lax.rsqrt(jnp.mean(yc * yc, axis=-1, keepdims=True) + EPS) * lng_ref[...] + lnb_ref[...]
    act = yn * _sigmoid(yn)
    mix = (_dot(act.astype(BF16), w2_ref[...]) + b2_ref[...]).reshape(TILE_CHUNKS, CHUNK, D_MODEL)
    x1_ref[...] = x_ref[...] + m[:, 2:3, :] * _rms(mix, gpost_ref[...])


def _conv(x, modc, hist0, params, *, chunk0, nseq_total, ntiles, nseq):
    tile0 = chunk0 // TILE_CHUNKS
    if ntiles == 1:
        cidx = lambda s, j: (tile0 + s, 0, 0)
        oidx = lambda s, j: (s, 0, 0)
    else:
        assert nseq == 1
        cidx = lambda s, j: (tile0 + s * ntiles + j, 0, 0)
        oidx = lambda s, j: (s * ntiles + j, 0, 0)
    full = lambda a: pl.BlockSpec(a.shape, lambda s, j: (0,) * a.ndim)
    hist_spec = pl.BlockSpec((nseq, HIST, D_MODEL), lambda s, j: (s, 0, 0))
    nblocks = nseq_total // nseq
    return pl.pallas_call(
        functools.partial(_conv_kernel, nseq=nseq, ntiles=ntiles),
        grid=(nblocks, ntiles),
        in_specs=[pl.BlockSpec((TILE_CHUNKS, CHUNK, D_MODEL), cidx),
                  pl.BlockSpec((TILE_CHUNKS, 6, D_MODEL), cidx),
                  hist_spec] + [full(a) for a in params],
        out_specs=[pl.BlockSpec((TILE_CHUNKS, CHUNK, D_MODEL), oidx), hist_spec],
        out_shape=[jax.ShapeDtypeStruct((nblocks * ntiles * TILE_CHUNKS, CHUNK, D_MODEL), F32),
                   jax.ShapeDtypeStruct((nseq_total, HIST, D_MODEL), F32)],
        scratch_shapes=[pltpu.VMEM((nseq, HIST + TILE // nseq, D_MODEL), F32),
                        pltpu.VMEM((7, TILE // nseq + HIST - 8, D_MODEL), F32)],
        compiler_params=_cparams(("arbitrary", "arbitrary")),
        name="conv",
    )(x, modc, hist0, *params)


def _moe_input(x_ref, mod_ref, g_ref):
    m = mod_ref[...]
    return _rms(x_ref[...], g_ref[...]) * (1.0 + m[:, 4:5, :]) + m[:, 3:4, :]


def _router_kernel(x_ref, mod_ref, g_ref, wrt_ref, br_ref,
                   gate_ref, gpos_ref, ce_ref, cbase_ref, nch_ref, cnt_ref, carry_scr):
    i = pl.program_id(0)

    @pl.when(i == 0)
    def _():
        carry_scr[...] = jnp.zeros_like(carry_scr)

    h2 = _moe_input(x_ref, mod_ref, g_ref).reshape(TILE, D_MODEL)
    h_hi = h2.astype(BF16)
    h_lo = (h2 - h_hi.astype(F32)).astype(BF16)
    w = wrt_ref[...]
    w_hi = w.astype(BF16)
    w_lo = (w - w_hi.astype(F32)).astype(BF16)
    logits = _dot_nt(w_hi, h_hi) + _dot_nt(w_lo, h_hi) + _dot_nt(w_hi, h_lo) + br_ref[...]
    e_iota = lax.broadcasted_iota(jnp.int32, (N_EXPERTS, TILE), 0)
    vals, sels = [], []
    lt = logits
    for _ in range(TOP_K):
        mx = jnp.max(lt, axis=0, keepdims=True)
        ix = jnp.min(jnp.where(lt == mx, e_iota, N_EXPERTS), axis=0, keepdims=True)
        sl = e_iota == ix
        vals.append(mx)
        sels.append(sl)
        lt = jnp.where(sl, -jnp.inf, lt)
    ex = [jnp.exp(v - vals[0]) for v in vals]
    den = ex[0] + ex[1] + ex[2] + ex[3]
    gate_ref[...] = _stack_rows([e / den for e in ex])
    chosen = (sels[0] | sels[1] | sels[2] | sels[3])
    chosen_f = jnp.where(chosen, 1.0, 0.0)
    rr = lax.broadcasted_iota(jnp.int32, (TILE, TILE), 0)
    cc = lax.broadcasted_iota(jnp.int32, (TILE, TILE), 1)
    before = (rr < cc).astype(BF16)
    local = _dot(chosen_f.astype(BF16), before)
    cnt = jnp.sum(chosen_f, axis=1, keepdims=True)
    padded = jnp.ceil(cnt * (1.0 / ROW_CHUNK)) * ROW_CHUNK
    er = lax.broadcasted_iota(jnp.int32, (N_EXPERTS, N_EXPERTS), 0)
    ec = lax.broadcasted_iota(jnp.int32, (N_EXPERTS, N_EXPERTS), 1)
    lower = (ec < er).astype(BF16)
    goff = _dot(lower, jnp.broadcast_to(padded, (N_EXPERTS, 128)).astype(BF16))[:, 0:1]
    gpos = [jnp.sum(jnp.where(sl, goff + local, 0.0), axis=0, keepdims=True) for sl in sels]
    gpos_ref[...] = _stack_rows(gpos).astype(jnp.int32)
    carry = carry_scr[...][:, 0:1]
    crow = lax.broadcasted_iota(jnp.int32, (N_EXPERTS, CHUNK_SLOTS), 1).astype(F32) * ROW_CHUNK
    ce = jnp.sum(jnp.where(goff + padded <= crow, 1.0, 0.0), axis=0, keepdims=True)
    ce = jnp.minimum(ce, N_EXPERTS - 1.0)
    e_f = lax.broadcasted_iota(jnp.int32, (N_EXPERTS, CHUNK_SLOTS), 0).astype(F32)
    cbase = jnp.sum(jnp.where(e_f == ce, carry - goff, 0.0), axis=0, keepdims=True) + crow[0:1]
    ce_ref[0] = ce.astype(jnp.int32)
    cbase_ref[0] = cbase.astype(jnp.int32)
    nch = jnp.sum(padded, axis=0, keepdims=True) * (1.0 / ROW_CHUNK)
    nch_ref[0] = jnp.broadcast_to(nch, (1, 128)).astype(jnp.int32)
    carry_scr[...] = carry_scr[...] + padded
    cnt_ref[...] = carry_scr[...]


def _router(x, modc, g_pre, w_router_t, b_router):
    nc = x.shape[0]
    nt = nc // TILE_CHUNKS
    n = nc * CHUNK
    full = lambda a: pl.BlockSpec(a.shape, lambda i: (0,) * a.ndim)
    tok = pl.BlockSpec((TILE_CHUNKS, CHUNK, D_MODEL), lambda i: (i, 0, 0))
    kt = pl.BlockSpec((TOP_K, TILE), lambda i: (0, i))
    tab = lambda w: pl.BlockSpec((1, 1, w), lambda i: (i, 0, 0))
    return pl.pallas_call(
        _router_kernel,
        grid=(nt,),
        in_specs=[tok, pl.BlockSpec((TILE_CHUNKS, 6, D_MODEL), lambda i: (i, 0, 0)),
                  full(g_pre), full(w_router_t), full(b_router)],
        out_specs=[kt, kt, tab(CHUNK_SLOTS), tab(CHUNK_SLOTS), tab(128),
                   pl.BlockSpec((N_EXPERTS, 128), lambda i: (0, 0))],
        out_shape=[jax.ShapeDtypeStruct((TOP_K, n), F32),
                   jax.ShapeDtypeStruct((TOP_K, n), jnp.int32),
                   jax.ShapeDtypeStruct((nt, 1, CHUNK_SLOTS), jnp.int32),
                   jax.ShapeDtypeStruct((nt, 1, CHUNK_SLOTS), jnp.int32),
                   jax.ShapeDtypeStruct((nt, 1, 128), jnp.int32),
                   jax.ShapeDtypeStruct((N_EXPERTS, 128), F32)],
        scratch_shapes=[pltpu.VMEM((N_EXPERTS, 128), F32)],
        compiler_params=_cparams(("arbitrary",)),
        name="router",
    )(x, modc, g_pre, w_router_t, b_router)


def _row_copy(src, dst, sem):
    return pltpu.make_async_copy(src, dst, sem)


def _dispatch_kernel(cdst_ref, nch_ref, pend_ref, x_ref, mod_ref, g_ref, gpos_ref, rows_ref,
                     grp_scr, zero_scr, sem, zsem):
    i = pl.program_id(0)
    nt = pl.num_programs(0)
    slot = i % 2

    @pl.when(i == 0)
    def _():
        zero_scr[...] = jnp.zeros_like(zero_scr)

        n_blocks = rows_ref.shape[0] // EXPERT_ROWS
        total = pend_ref[N_EXPERTS - 1]

        def block_to_clear(e):
            is_tail = e >= N_EXPERTS
            ee = jnp.minimum(e, N_EXPERTS - 1)
            end = pend_ref[ee]
            start = jnp.where(ee == 0, 0, pend_ref[jnp.maximum(ee - 1, 0)])
            tail_row = total + (e - N_EXPERTS) * EXPERT_ROWS
            row0 = jnp.where(is_tail, tail_row, jnp.maximum(end - EXPERT_ROWS, 0))
            needed = jnp.where(is_tail, tail_row < n_blocks * EXPERT_ROWS, end > start)
            row0 = jnp.minimum(row0, (n_blocks - 1) * EXPERT_ROWS)
            dst = rows_ref.at[pl.ds(pl.multiple_of(row0, EXPERT_ROWS), EXPERT_ROWS)]
            return needed, _row_copy(zero_scr, dst, zsem)

        def clear(e, c):
            needed, cp = block_to_clear(e)

            @pl.when(needed)
            def _():
                cp.start()
            return c

        def settle(e, c):
            needed, cp = block_to_clear(e)

            @pl.when(needed)
            def _():
                cp.wait()
            return c

        n_tail = n_blocks - (TILE * TOP_K * nt) // EXPERT_ROWS
        lax.fori_loop(0, N_EXPERTS + n_tail, clear, 0)
        lax.fori_loop(0, N_EXPERTS + n_tail, settle, 0)

    h = _moe_input(x_ref, mod_ref, g_ref).reshape(TILE, D_MODEL).astype(BF16)
    gp = gpos_ref[...]
    jr = lax.broadcasted_iota(jnp.int32, (GROUP_ROWS, TILE), 0)
    hit = jr == gp[0:1]
    for k in range(1, TOP_K):
        hit = hit | (jr == gp[k:k + 1])
    grp_scr[slot] = _dot(jnp.where(hit, 1.0, 0.0).astype(BF16), h)

    def chunk_copy(tile, sl, c):
        d = cdst_ref[tile * CHUNK_SLOTS + c]
        src = grp_scr.at[sl, pl.ds(pl.multiple_of(c * ROW_CHUNK, ROW_CHUNK), ROW_CHUNK)]
        return _row_copy(src, rows_ref.at[pl.ds(pl.multiple_of(d, ROW_CHUNK), ROW_CHUNK)], sem.at[sl])

    n_own = nch_ref[i]

    def issue(c2, carry):
        c = 2 * c2
        chunk_copy(i, slot, c).start(priority=0)

        @pl.when(c + 1 < n_own)
        def _():
            chunk_copy(i, slot, c + 1).start(priority=1)
        return carry

    lax.fori_loop(0, (n_own + 1) // 2, issue, 0)

    def drain(tile, sl):
        n_all = nch_ref[tile]
        many = WAIT_CHUNKS * ROW_CHUNK
        bulk = _row_copy(grp_scr.at[sl, pl.ds(0, many)], rows_ref.at[pl.ds(0, many)], sem.at[sl])

        def bulk_wait(c, carry):
            bulk.wait()
            return carry

        def single_wait(c, carry):
            chunk_copy(tile, sl, 0).wait()
            return carry

        lax.fori_loop(0, n_all // WAIT_CHUNKS, bulk_wait, 0)
        lax.fori_loop(0, n_all % WAIT_CHUNKS, single_wait, 0)

    @pl.when(i > 0)
    def _():
        drain(i - 1, 1 - slot)

    @pl.when(i == nt - 1)
    def _():
        drain(i, slot)


def _dispatch(chunk_dst, n_chunks, p_ends, x, modc, g_pre, gpos_t, n_rows):
    nc = x.shape[0]
    idx3 = lambda i, cd, nch, pend: (i, 0, 0)
    return pl.pallas_call(
        _dispatch_kernel,
        grid_spec=pltpu.PrefetchScalarGridSpec(
            num_scalar_prefetch=3,
            grid=(nc // TILE_CHUNKS,),
            in_specs=[pl.BlockSpec((TILE_CHUNKS, CHUNK, D_MODEL), idx3),
                      pl.BlockSpec((TILE_CHUNKS, 6, D_MODEL), idx3),
                      pl.BlockSpec(g_pre.shape, lambda i, cd, nch, pend: (0, 0)),
                      pl.BlockSpec((TOP_K, TILE), lambda i, cd, nch, pend: (0, i))],
            out_specs=pl.BlockSpec(memory_space=pl.ANY),
            scratch_shapes=[pltpu.VMEM((2, GROUP_ROWS, D_MODEL), F32), pltpu.VMEM((EXPERT_ROWS, D_MODEL), F32),
                            pltpu.SemaphoreType.DMA((2,)), pltpu.SemaphoreType.DMA],
        ),
        out_shape=jax.ShapeDtypeStruct((n_rows, D_MODEL), F32),
        compiler_params=_cparams(("arbitrary",)),
        name="dispatch",
    )(chunk_dst, n_chunks, p_ends, x, modc, g_pre, gpos_t)


def _experts_kernel(be_ref, nu_ref, run_ref, nxt_ref, x_ref, b1_ref, b2_ref, w1_hbm, w2_hbm, y_ref,
                    w1_f32, w2_f32, w1_scr, w2_scr, sem, *, layer):
    i = pl.program_id(0)
    valid = i < nu_ref[0]
    e = be_ref[i]
    changed = jnp.logical_or(i == 0, e != be_ref[jnp.maximum(i - 1, 0)])
    slot = run_ref[i] % 2
    dff = w2_scr.shape[0]

    def fetch(expert, sl):
        return (_row_copy(w1_hbm.at[layer, expert], w1_f32.at[sl], sem.at[sl, 0]),
                _row_copy(w2_hbm.at[layer, expert], w2_f32.at[sl], sem.at[sl, 1]))

    @pl.when(i == 0)
    def _():
        for cp in fetch(e, 0):
            cp.start()

    @pl.when(jnp.logical_and(valid, changed))
    def _():
        nxt = nxt_ref[i]

        @pl.when(nxt >= 0)
        def _():
            for cp in fetch(nxt, 1 - slot):
                cp.start()

        for cp in fetch(e, slot):
            cp.wait()
        rows = 128

        def cast(c, carry):
            r0 = pl.multiple_of(c * rows, rows)
            w1_scr[pl.ds(r0, rows), :] = w1_f32[slot, pl.ds(r0, rows), :].astype(BF16)
            w2_scr[pl.ds(r0, rows), :] = w2_f32[slot, pl.ds(r0, rows), :].astype(BF16)
            return carry

        lax.fori_loop(0, D_MODEL // rows, cast, 0)

    @pl.when(valid)
    def _():
        h = _dot(x_ref[...].astype(BF16), w1_scr[...]) + b1_ref[0, 0]
        gate = jnp.minimum(h[:, :dff], SWIGLU_LIMIT)
        up = jnp.clip(h[:, dff:], -SWIGLU_LIMIT, SWIGLU_LIMIT)
        act = (up + 1.0) * (gate * _sigmoid(SWIGLU_ALPHA * gate))
        y_ref[...] = _dot(act.astype(BF16), w2_scr[...]) + b2_ref[0, 0]

    @pl.when(jnp.logical_not(valid))
    def _():
        y_ref[...] = jnp.zeros_like(y_ref)


def _experts(blk_e, n_used, blk_run, blk_next, x_rows, w1, b1, w2, b2, layer):
    n_rows = x_rows.shape[0]
    n_blocks = n_rows // EXPERT_ROWS
    nl, e, d, f2 = w1.shape
    dff = w2.shape[2]
    assert d == dff == D_MODEL
    row_idx = lambda i, be, nu, run, nxt: (jnp.minimum(i, nu[0] - 1), 0)
    b_idx = lambda i, be, nu, run, nxt: (layer, be[i], 0, 0)
    return pl.pallas_call(
        functools.partial(_experts_kernel, layer=layer),
        grid_spec=pltpu.PrefetchScalarGridSpec(
            num_scalar_prefetch=4,
            grid=(n_blocks,),
            in_specs=[pl.BlockSpec((EXPERT_ROWS, D_MODEL), row_idx),
                      pl.BlockSpec((1, 1, 1, f2), b_idx),
                      pl.BlockSpec((1, 1, 1, D_MODEL), b_idx),
                      pl.BlockSpec(memory_space=pl.ANY),
                      pl.BlockSpec(memory_space=pl.ANY)],
            out_specs=pl.BlockSpec((EXPERT_ROWS, D_MODEL), lambda i, be, nu, run, nxt: (i, 0)),
            scratch_shapes=[pltpu.VMEM((2, d, f2), F32), pltpu.VMEM((2, dff, D_MODEL), F32),
                            pltpu.VMEM((d, f2), BF16), pltpu.VMEM((dff, D_MODEL), BF16),
                            pltpu.SemaphoreType.DMA((2, 2))],
        ),
        out_shape=jax.ShapeDtypeStruct((n_rows, D_MODEL), F32),
        compiler_params=_cparams(("arbitrary",), EXPERTS_VMEM_LIMIT),
        name="experts",
    )(blk_e, n_used, blk_run, blk_next, x_rows, b1.reshape(nl, e, 1, f2), b2.reshape(nl, e, 1, D_MODEL), w1, w2)


def _combine_kernel(cdst_ref, nch_ref, gate_ref, gpos_ref, x_ref, mod_ref, gpost_ref, rows_ref, *rest, ntp):
    o_refs, (ybuf, sem) = rest[:-2], rest[-2:]
    i = pl.program_id(0)
    nt = pl.num_programs(0)
    slot = i % 2

    def chunk_copy(tile, sl, c):
        d = cdst_ref[tile * CHUNK_SLOTS + c]
        dst = ybuf.at[sl, pl.ds(pl.multiple_of(c * ROW_CHUNK, ROW_CHUNK), ROW_CHUNK)]
        return _row_copy(rows_ref.at[pl.ds(pl.multiple_of(d, ROW_CHUNK), ROW_CHUNK)], dst, sem.at[sl])

    def gather_tile(tile, sl):
        n_tile = nch_ref[tile]

        def issue(c2, carry):
            c = 2 * c2
            chunk_copy(tile, sl, c).start(priority=0)

            @pl.when(c + 1 < n_tile)
            def _():
                chunk_copy(tile, sl, c + 1).start(priority=1)
            return carry

        lax.fori_loop(0, (n_tile + 1) // 2, issue, 0)

    @pl.when(i == 0)
    def _():
        ybuf[...] = jnp.zeros_like(ybuf)
        gather_tile(0, 0)

    @pl.when(i + 1 < nt)
    def _():
        gather_tile(i + 1, 1 - slot)

    n_own = nch_ref[i]
    many = WAIT_CHUNKS * ROW_CHUNK
    bulk = _row_copy(rows_ref.at[pl.ds(0, many)], ybuf.at[slot, pl.ds(0, many)], sem.at[slot])

    def bulk_wait(c, carry):
        bulk.wait()
        return carry

    def single_wait(c, carry):
        chunk_copy(i, slot, 0).wait()
        return carry

    lax.fori_loop(0, n_own // WAIT_CHUNKS, bulk_wait, 0)
    lax.fori_loop(0, n_own % WAIT_CHUNKS, single_wait, 0)

    g = gate_ref[...]
    gp = gpos_ref[...]
    jc = lax.broadcasted_iota(jnp.int32, (TILE, GROUP_ROWS), 1)
    gm = jnp.where(jc == gp[:, 0:1], g[:, 0:1], 0.0)
    for k in range(1, TOP_K):
        gm = gm + jnp.where(jc == gp[:, k:k + 1], g[:, k:k + 1], 0.0)
    gm_hi = gm.astype(BF16)
    gm_lo = (gm - gm_hi.astype(F32)).astype(BF16)
    yb = ybuf[slot].astype(BF16)
    y = _dot(gm_hi, yb) + _dot(gm_lo, yb)
    m = mod_ref[...]
    y = y.reshape(TILE_CHUNKS, CHUNK, D_MODEL)
    out = x_ref[...] + m[:, 5:6, :] * _rms(y, gpost_ref[...])
    if ntp is None:
        o_refs[0][...] = out
    else:
        @pl.when(i < ntp)
        def _():
            o_refs[0][...] = out

        @pl.when(i >= ntp)
        def _():
            o_refs[1][...] = out


def _combine(chunk_dst, n_chunks, gates, gpos, x, modc, g_post, y_rows, split_chunks=None):
    nc = x.shape[0]
    ntp = None if split_chunks is None else split_chunks // TILE_CHUNKS
    idx3 = lambda i, cd, nch: (i, 0, 0)
    tok = pl.BlockSpec((TILE_CHUNKS, CHUNK, D_MODEL), idx3)
    per_tok = pl.BlockSpec((TILE, TOP_K), lambda i, cd, nch: (i, 0))
    if ntp is None:
        out_specs, out_shape = tok, jax.ShapeDtypeStruct(x.shape, F32)
    else:
        out_specs = list(_two_part_specs(ntp, lambda f: pl.BlockSpec((TILE_CHUNKS, CHUNK, D_MODEL), f)))
        out_shape = [jax.ShapeDtypeStruct((split_chunks, CHUNK, D_MODEL), F32),
                     jax.ShapeDtypeStruct((nc - split_chunks, CHUNK, D_MODEL), F32)]
    return pl.pallas_call(
        functools.partial(_combine_kernel, ntp=ntp),
        grid_spec=pltpu.PrefetchScalarGridSpec(
            num_scalar_prefetch=2,
            grid=(nc // TILE_CHUNKS,),
            in_specs=[per_tok, per_tok, tok,
                      pl.BlockSpec((TILE_CHUNKS, 6, D_MODEL), idx3),
                      pl.BlockSpec(g_post.shape, lambda i, cd, nch: (0, 0)),
                      pl.BlockSpec(memory_space=pl.ANY)],
            out_specs=out_specs,
            scratch_shapes=[pltpu.VMEM((2, GROUP_ROWS, D_MODEL), F32), pltpu.SemaphoreType.DMA((2,))],
        ),
        out_shape=out_shape,
        compiler_params=_cparams(("arbitrary",)),
        name="combine",
    )(chunk_dst, n_chunks, gates, gpos, x, modc, g_post, y_rows)


def _moe_layer(x, modc, g_pre, g_post, w_router, b_router, w1, b1, w2, b2, layer, split_chunks=None):
    nc = x.shape[0]
    n = nc * CHUNK
    nt = n // TILE
    gate_t, gpos_t, chunk_e, chunk_base, n_chunks, cnt = _router(
        x, modc, g_pre, w_router.T, b_router.reshape(N_EXPERTS, 1))
    counts = cnt[:, 0].astype(jnp.int32)
    padded = (counts + EXPERT_ROWS - 1) // EXPERT_ROWS * EXPERT_ROWS
    p_ends = jnp.cumsum(padded).astype(jnp.int32)
    p_starts = p_ends - padded
    max_rows = n * TOP_K + nt * N_EXPERTS * (ROW_CHUNK - 1)
    n_blocks = -(-max_rows // EXPERT_ROWS) + N_EXPERTS
    n_used = p_ends[-1] // EXPERT_ROWS
    blk = jnp.minimum(jnp.arange(n_blocks, dtype=jnp.int32), n_used - 1) * EXPERT_ROWS
    blk_e = jnp.minimum(jnp.sum((p_ends[None, :] <= blk[:, None]).astype(jnp.int32), axis=1), N_EXPERTS - 1)
    eids = jnp.arange(N_EXPERTS, dtype=jnp.int32)
    chunk_e = chunk_e.reshape(nt, CHUNK_SLOTS)
    chunk_dst = jnp.sum(jnp.where(chunk_e[:, :, None] == eids, p_starts, 0), axis=-1) + chunk_base.reshape(nt, CHUNK_SLOTS)
    chunk_dst = chunk_dst.reshape(-1)
    n_chunks = n_chunks[:, 0, 0]
    blk_run = jnp.cumsum(jnp.concatenate([jnp.ones((1,), jnp.int32),
                                          (blk_e[1:] != blk_e[:-1]).astype(jnp.int32)])) - 1
    later_used = (eids[None, :] > eids[:, None]) & (counts > 0)[None, :]
    next_used = jnp.min(jnp.where(later_used, eids[None, :], N_EXPERTS), axis=1)
    next_used = jnp.where(next_used == N_EXPERTS, -1, next_used)
    blk_next = jnp.sum(jnp.where(blk_e[:, None] == eids, next_used, 0), axis=1)
    x_rows = _dispatch(chunk_dst, n_chunks, p_ends, x, modc, g_pre, gpos_t, n_blocks * EXPERT_ROWS)
    y_rows = _experts(blk_e, n_used.reshape(1), blk_run, blk_next, x_rows, w1, b1, w2, b2, layer)
    return _combine(chunk_dst, n_chunks, gate_t.T, gpos_t.T, x, modc, g_post, y_rows, split_chunks)


def kernel(x_prompt, x_sample, state_gla, cache_conv, c_prompt, c_sample, w_ada, b_ada, norm_pre, norm_post, w_gla_in, w_gla_gate_up, b_gla_gate, g_gla_norm, w_gla_out, w_pw1, b_pw1, w_dw, b_dw, ln_g, ln_b, w_pw2, b_pw2, w_router, b_router, w_e1, b_e1, w_e2, b_e2):
    bp, lp, d = x_prompt.shape
    bs, ls, _ = x_sample.shape
    depth = w_ada.shape[0]
    assert d == D_MODEL and ls == CHUNK and lp % TILE == 0 and bs % TILE_CHUNKS == 0
    assert bp == GLA_SEQS and bs % GLA_SEQS == 0
    ncp = bp * lp // CHUNK
    ncs = bs
    nseq = bp + bs
    nch = lp // CHUNK

    x_p0 = x_prompt.reshape(ncp, CHUNK, d)
    x_s0 = x_sample.reshape(ncs, CHUNK, d)
    x = None
    rows = -(-nseq // 8) * 8
    c_all = jnp.concatenate([c_prompt, c_sample, jnp.zeros((rows - nseq, d), F32)], axis=0)
    mod = _ada(c_all, w_ada, b_ada)
    mod_p = jnp.broadcast_to(mod[:, :bp, None, :], (depth, bp, nch, 6 * d)).reshape(depth, ncp, 6 * d)
    modc = jnp.concatenate([mod_p, mod[:, bp:nseq]], axis=1).reshape(depth, ncp + ncs, 6, d)

    row = lambda a: a.reshape(1, -1)
    dk = GLA_HEADS * DKH
    dv = GLA_HEADS * DVH
    gla_states, conv_hists = [], []
    for i in range(depth):
        j = i // 2
        if i % 2 == 0:
            w_in = w_gla_in[j]
            rank = w_gla_gate_up.shape[1]
            w_main = w_in[:, :2 * dk + 2 * dv].astype(BF16)
            w_gz = jnp.pad(w_in[:, 2 * dk + 2 * dv:], ((0, 0), (0, 128 - rank))).astype(BF16)
            w_gu = jnp.pad(w_gla_gate_up[j], ((0, 128 - rank), (0, 0))).astype(BF16)
            assert i == 0
            q, k, v, r, lg = _gla_in(x_p0, x_s0, modc[i], row(norm_pre[i, 0]), w_main, w_gz, w_gu, row(b_gla_gate[j]))
            s0_p = jnp.zeros((bp, GLA_HEADS, DVH, DKH), F32)
            s0_s = jnp.swapaxes(state_gla[j], -1, -2)
            o_p, st_p = _gla_core(q, k, v, lg, s0_p, chunk0=0, nblocks=1, nchunks=nch)
            o_s, st_s = _gla_core(q, k, v, lg, s0_s, chunk0=ncp, nblocks=bs // GLA_SEQS, nchunks=1)
            gla_states.append((jnp.swapaxes(st_p, -1, -2), jnp.swapaxes(st_s, -1, -2)))
            x = _gla_out(o_p, o_s, r, x_p0, x_s0, modc[i], row(g_gla_norm[j]), w_gla_out[j].astype(BF16),
                         row(norm_post[i, 0]))
        else:
            params = (row(norm_pre[i, 0]), w_pw1[j].astype(BF16), row(b_pw1[j]), w_dw[j], row(b_dw[j]),
                      row(ln_g[j]), row(ln_b[j]), w_pw2[j].astype(BF16), row(b_pw2[j]), row(norm_post[i, 0]))
            keep = CONV_WIDTH - 1
            h0_p = jnp.zeros((bp, HIST, d), F32)
            h0_s = jnp.pad(cache_conv[j], ((0, 0), (HIST - keep, 0), (0, 0)))
            x_p, hp = _conv(x, modc[i], h0_p, params, chunk0=0, nseq_total=bp, ntiles=lp // TILE, nseq=1)
            x_s, hs = _conv(x, modc[i], h0_s, params, chunk0=ncp, nseq_total=bs, ntiles=1, nseq=TILE_CHUNKS)
            x = jnp.concatenate([x_p, x_s], axis=0)
            conv_hists.append((hp[:, HIST - keep:], hs[:, HIST - keep:]))
        x = _moe_layer(x, modc[i], row(norm_pre[i, 1]), row(norm_post[i, 1]),
                       w_router[i], b_router[i], w_e1, b_e1, w_e2, b_e2, i,
                       split_chunks=ncp if i == depth - 1 else None)

    y_prompt = x[0].reshape(bp, lp, d)
    y_sample = x[1].reshape(bs, ls, d)
    gla_p = jnp.stack([s[0] for s in gla_states])
    gla_s = jnp.stack([s[1] for s in gla_states])
    conv_p = jnp.stack([c[0] for c in conv_hists])
    conv_s = jnp.stack([c[1] for c in conv_hists])
    return (y_prompt, y_sample, gla_p, gla_s, conv_p, conv_s)
```

```python
import functools

import jax
import jax.numpy as jnp
from jax import lax
from jax.experimental import pallas as pl
from jax.experimental.pallas import tpu as pltpu

F32 = jnp.float32
BF16 = jnp.bfloat16

D_MODEL = 1024
CHUNK = 64
TILE_CHUNKS = 4
TILE = CHUNK * TILE_CHUNKS
GLA_HEADS = 4
DKH = 128
DVH = 256
SUB = 16
GLA_SEQS = 4
GATE_TAU = 16.0
CONV_WIDTH = 31
HIST = 32
N_EXPERTS = 32
TOP_K = 4
SWIGLU_LIMIT = 7.0
SWIGLU_ALPHA = 1.702
EXPERT_ROWS = 512
ROW_CHUNK = 8
GROUP_ROWS = TILE * TOP_K + N_EXPERTS * ROW_CHUNK
CHUNK_SLOTS = 256
WAIT_CHUNKS = 16
EPS = 1e-6
VMEM_LIMIT = 48 * 1024 * 1024
EXPERTS_VMEM_LIMIT = 56 * 1024 * 1024


def _cparams(sem, vmem=VMEM_LIMIT):
    return pltpu.CompilerParams(dimension_semantics=sem, vmem_limit_bytes=vmem)


def _dot(a, b):
    return jnp.dot(a, b, preferred_element_type=F32)


def _dot_nt(a, b):
    return lax.dot_general(a, b, (((1,), (1,)), ((), ())), preferred_element_type=F32)


def _dot_tn(a, b):
    return lax.dot_general(a, b, (((0,), (0,)), ((), ())), preferred_element_type=F32)


def _sigmoid(x):
    return 1.0 / (1.0 + jnp.exp(-x))


def _rms(x, g):
    return x * lax.rsqrt(jnp.mean(x * x, axis=-1, keepdims=True) + EPS) * g


def _stack_rows(rows):
    n = rows[0].shape[1]
    ri = lax.broadcasted_iota(jnp.int32, (len(rows), n), 0)
    out = jnp.broadcast_to(rows[0], (len(rows), n))
    for j in range(1, len(rows)):
        out = jnp.where(ri == j, jnp.broadcast_to(rows[j], (len(rows), n)), out)
    return out


def _split3(x):
    hi = x.astype(BF16)
    r1 = x - hi.astype(F32)
    mid = r1.astype(BF16)
    lo = (r1 - mid.astype(F32)).astype(BF16)
    return hi, mid, lo


def _two_part_specs(ntp, make):
    return (make(lambda i, *_: (jnp.minimum(i, ntp - 1), 0, 0)),
            make(lambda i, *_: (jnp.maximum(i - ntp, 0), 0, 0)))


def _pick_part(i, ntp, first_ref, second_ref):
    return jnp.where(i < ntp, first_ref[...], second_ref[...])


def _ada_kernel(c_ref, w_ref, b_ref, o_ref):
    c = c_ref[...]
    cs = (c * _sigmoid(c)).astype(BF16)
    o_ref[0] = _dot(cs, w_ref[0].astype(BF16)) + b_ref[0]


def _ada(c_all, w_ada, b_ada):
    depth, d, n6 = w_ada.shape
    rows = c_all.shape[0]
    nt = n6 // d
    return pl.pallas_call(
        _ada_kernel,
        grid=(depth, nt),
        in_specs=[
            pl.BlockSpec((rows, d), lambda i, n: (0, 0)),
            pl.BlockSpec((1, d, d), lambda i, n: (i, 0, n)),
            pl.BlockSpec((1, 1, d), lambda i, n: (i, 0, n)),
        ],
        out_specs=pl.BlockSpec((1, rows, d), lambda i, n: (i, 0, n)),
        out_shape=jax.ShapeDtypeStruct((depth, rows, n6), F32),
        compiler_params=_cparams(("arbitrary", "arbitrary")),
        name="ada",
    )(c_all, w_ada, b_ada.reshape(depth, 1, n6))


def _gla_in_kernel(xp_ref, xs_ref, mod_ref, g_ref, w_ref, wgz_ref, wgu_ref, bg_ref,
                   q_ref, k_ref, v_ref, r_ref, lg_ref, *, ntp):
    m = mod_ref[...]
    x = _pick_part(pl.program_id(0), ntp, xp_ref, xs_ref)
    h = _rms(x, g_ref[...]) * (1.0 + m[:, 1:2, :]) + m[:, 0:1, :]
    hb = h.reshape(TILE, D_MODEL).astype(BF16)
    p = _dot(hb, w_ref[...])
    dk = GLA_HEADS * DKH
    dv = GLA_HEADS * DVH
    q_ref[...] = (p[:, :dk] * (DKH ** -0.5)).reshape(TILE_CHUNKS, CHUNK, dk)
    k_ref[...] = p[:, dk:2 * dk].reshape(TILE_CHUNKS, CHUNK, dk)
    v_ref[...] = p[:, 2 * dk:2 * dk + dv].reshape(TILE_CHUNKS, CHUNK, dv)
    r_ref[...] = p[:, 2 * dk + dv:].reshape(TILE_CHUNKS, CHUNK, dv)
    gz = _dot(hb, wgz_ref[...])
    z = _dot(gz.astype(BF16), wgu_ref[...]) + bg_ref[...]
    ls = jnp.minimum(z, 0.0) - jnp.log(1.0 + jnp.exp(-jnp.abs(z)))
    lg_ref[...] = (ls / GATE_TAU).reshape(TILE_CHUNKS, CHUNK, dk)


def _gla_in(x_p, x_s, modc, g_pre, w_main, w_gz, w_gu, b_gate):
    nc = x_p.shape[0] + x_s.shape[0]
    ntp = x_p.shape[0] // TILE_CHUNKS
    dk = GLA_HEADS * DKH
    dv = GLA_HEADS * DVH
    tok = lambda w: pl.BlockSpec((TILE_CHUNKS, CHUNK, w), lambda i: (i, 0, 0))
    full = lambda a: pl.BlockSpec(a.shape, lambda i: (0,) * a.ndim)
    outs = [jax.ShapeDtypeStruct((nc, CHUNK, w), F32) for w in (dk, dk, dv, dv, dk)]
    xspecs = _two_part_specs(ntp, lambda f: pl.BlockSpec((TILE_CHUNKS, CHUNK, D_MODEL), f))
    return pl.pallas_call(
        functools.partial(_gla_in_kernel, ntp=ntp),
        grid=(nc // TILE_CHUNKS,),
        in_specs=[*xspecs, pl.BlockSpec((TILE_CHUNKS, 6, D_MODEL), lambda i: (i, 0, 0)),
                  full(g_pre), full(w_main), full(w_gz), full(w_gu), full(b_gate)],
        out_specs=[tok(dk), tok(dk), tok(dv), tok(dv), tok(dk)],
        out_shape=outs,
        compiler_params=_cparams(("arbitrary",)),
        name="gla_in",
    )(x_p, x_s, modc, g_pre, w_main, w_gz, w_gu, b_gate)


def _gla_core_kernel(*refs, nchunks):
    ns = GLA_SEQS
    q_refs, k_refs, v_refs, lg_refs = (refs[g * ns:(g + 1) * ns] for g in range(4))
    s0_ref = refs[4 * ns]
    o_refs = refs[4 * ns + 1:5 * ns + 1]
    sout_ref, s_scr = refs[5 * ns + 1], refs[5 * ns + 2]
    j = pl.program_id(1)

    @pl.when(j == 0)
    def _():
        s_scr[...] = s0_ref[...]

    nsub = CHUNK // SUB
    npair = SUB * SUB
    rr = lax.broadcasted_iota(jnp.int32, (CHUNK, CHUNK), 0)
    cc = lax.broadcasted_iota(jnp.int32, (CHUNK, CHUNK), 1)
    tril = (cc <= rr).astype(BF16)
    ones_k = jnp.ones((DKH, DKH), BF16)
    pr = lax.broadcasted_iota(jnp.int32, (nsub * npair, 2 * CHUNK), 0)
    pc = lax.broadcasted_iota(jnp.int32, (nsub * npair, 2 * CHUNK), 1)
    ps, pt, pi = pr % SUB, (pr // SUB) % SUB, pr // npair
    place = ((pc == SUB * pi + ps) & (ps <= pt)).astype(F32)
    st = lax.broadcasted_iota(jnp.int32, (CHUNK, nsub * npair), 0)
    sp = lax.broadcasted_iota(jnp.int32, (CHUNK, nsub * npair), 1)
    sel = (sp // SUB == st).astype(BF16)
    row_sub = lax.broadcasted_iota(jnp.int32, (CHUNK, DKH), 0) // SUB
    key_row = lax.broadcasted_iota(jnp.int32, (CHUNK, DKH), 0)
    zeros_v = jnp.zeros((CHUNK, DVH), BF16)

    for b in range(ns):
        lg = lg_refs[b][0]
        hi, mid, lo = _split3(lg)
        bcum = _dot(tril, hi) + _dot(tril, mid) + _dot(tril, lo)
        q_all = q_refs[b][0]
        k_all = k_refs[b][0]
        v_all = v_refs[b][0]
        heads = range(GLA_HEADS)
        qs = [q_all[:, h * DKH:(h + 1) * DKH] for h in heads]
        ks = [k_all[:, h * DKH:(h + 1) * DKH] for h in heads]
        bs_ = [bcum[:, h * DKH:(h + 1) * DKH] for h in heads]
        vbs = [v_all[:, h * DVH:(h + 1) * DVH].astype(BF16) for h in heads]
        outs = []
        for h in heads:
            blast = bs_[h][CHUNK - 1:CHUNK, :]
            s_t = s_scr[b, h]
            outs.append(_dot_nt((qs[h] * jnp.exp(bs_[h])).astype(BF16), s_t.astype(BF16)))
            kdec = (ks[h] * jnp.exp(blast - bs_[h])).astype(BF16)
            s_scr[b, h] = s_t * jnp.exp(blast) + _dot_tn(vbs[h], kdec)
        pairs = []
        for h in heads:
            qh, kh, bh = qs[h], ks[h], bs_[h]
            pieces = []
            for t in range(CHUNK):
                nk = SUB // 2 if t % SUB < SUB // 2 else SUB
                k0 = t // SUB * SUB
                qt = jnp.broadcast_to(qh[t:t + 1], (nk, DKH))
                bq = jnp.broadcast_to(bh[t:t + 1], (nk, DKH))
                piece = qt * kh[k0:k0 + nk] * jnp.exp(jnp.minimum(bq - bh[k0:k0 + nk], 0.0))
                if nk < SUB:
                    piece = jnp.concatenate([piece, jnp.zeros((SUB - nk, DKH), F32)], axis=0)
                pieces.append(piece)
            pairs.append(jnp.concatenate(pieces, axis=0).astype(BF16))
        rsums = [_dot(pairs[h], ones_k) for h in heads]
        a_diag = [_dot(sel, (rsums[h] * place).astype(BF16)) for h in heads]
        a_off = []
        for h in heads:
            qh, kh, bh = qs[h], ks[h], bs_[h]
            qd, kd = [], []
            for i in range(1, nsub):
                anc = bh[SUB * i - 1:SUB * i, :]
                qd.append(jnp.where(row_sub == i, qh * jnp.exp(jnp.minimum(bh - anc, 0.0)), 0.0))
                kd.append(jnp.where(key_row < SUB * i, kh * jnp.exp(jnp.minimum(anc - bh, 0.0)), 0.0))
            qd = jnp.concatenate(qd, axis=-1).astype(BF16)
            kd = jnp.concatenate(kd, axis=-1).astype(BF16)
            a_off.append(_dot_nt(qd, jnp.concatenate([kd, jnp.zeros_like(kd)], axis=0)))
        for h in heads:
            a = (a_diag[h] + a_off[h]).astype(BF16)
            o = outs[h] + _dot(a, jnp.concatenate([vbs[h], zeros_v], axis=0))
            o_refs[b][0, :, h * DVH:(h + 1) * DVH] = o

    @pl.when(j == nchunks - 1)
    def _():
        sout_ref[...] = s_scr[...]


def _gla_core(q, k, v, lg, s0_t, *, chunk0, nblocks, nchunks):
    dk = GLA_HEADS * DKH
    dv = GLA_HEADS * DVH
    ns = GLA_SEQS

    def tok(w, g):
        return pl.BlockSpec((1, CHUNK, w), lambda s, j: (chunk0 + (s * ns + g) * nchunks + j, 0, 0))

    st_spec = pl.BlockSpec((ns, GLA_HEADS, DVH, DKH), lambda s, j: (s, 0, 0, 0))
    o_spec = pl.BlockSpec((1, CHUNK, dv), lambda s, j: (s * nchunks + j, 0, 0))
    in_specs = [tok(w, g) for w in (dk, dk, dv, dk) for g in range(ns)] + [st_spec]
    args = [a for a in (q, k, v, lg) for _ in range(ns)] + [s0_t]
    res = pl.pallas_call(
        functools.partial(_gla_core_kernel, nchunks=nchunks),
        grid=(nblocks, nchunks),
        in_specs=in_specs,
        out_specs=[o_spec] * ns + [st_spec],
        out_shape=[jax.ShapeDtypeStruct((nblocks * nchunks, CHUNK, dv), F32)] * ns
        + [jax.ShapeDtypeStruct((nblocks * ns, GLA_HEADS, DVH, DKH), F32)],
        scratch_shapes=[pltpu.VMEM((ns, GLA_HEADS, DVH, DKH), F32)],
        compiler_params=_cparams(("arbitrary", "arbitrary")),
        name="gla_core",
    )(*args)
    return res[:ns], res[ns]


def _gla_out_kernel(*refs, ntp, tiles_per_seq):
    op_refs, os_refs = refs[:GLA_SEQS], refs[GLA_SEQS:2 * GLA_SEQS]
    r_ref, xp_ref, xs_ref, mod_ref, gn_ref, w_ref, gpost_ref, x1_ref = refs[2 * GLA_SEQS:]
    i = pl.program_id(0)
    o = jnp.concatenate([ref[...] for ref in os_refs], axis=0)
    for g in reversed(range(GLA_SEQS)):
        o = jnp.where(i < (g + 1) * tiles_per_seq, op_refs[g][...], o)
    o = o.reshape(TILE, GLA_HEADS * DVH)
    r = r_ref[...].reshape(TILE, GLA_HEADS * DVH)
    gn = gn_ref[...]
    parts = [_rms(o[:, h * DVH:(h + 1) * DVH], gn) for h in range(GLA_HEADS)]
    y = jnp.concatenate(parts, axis=-1) * (r * _sigmoid(r))
    mix = _dot(y.astype(BF16), w_ref[...]).reshape(TILE_CHUNKS, CHUNK, D_MODEL)
    m = mod_ref[...]
    x = _pick_part(pl.program_id(0), ntp, xp_ref, xs_ref)
    x1_ref[...] = x + m[:, 2:3, :] * _rms(mix, gpost_ref[...])


def _gla_out(o_p, o_s, r, x_p, x_s, modc, g_norm, w_out, g_post):
    nc = x_p.shape[0] + x_s.shape[0]
    ntp = x_p.shape[0] // TILE_CHUNKS
    assert len(o_p) == len(o_s) == GLA_SEQS == TILE_CHUNKS
    tps = ntp // GLA_SEQS
    dv = GLA_HEADS * DVH
    op_specs = [pl.BlockSpec((TILE_CHUNKS, CHUNK, dv), lambda i, g=g: (jnp.clip(i - g * tps, 0, tps - 1), 0, 0))
                for g in range(GLA_SEQS)]
    os_specs = [pl.BlockSpec((1, CHUNK, dv), lambda i: (jnp.maximum(i - ntp, 0), 0, 0))] * GLA_SEQS
    tok = lambda w: pl.BlockSpec((TILE_CHUNKS, CHUNK, w), lambda i: (i, 0, 0))
    full = lambda a: pl.BlockSpec(a.shape, lambda i: (0,) * a.ndim)
    xspecs = _two_part_specs(ntp, lambda f: pl.BlockSpec((TILE_CHUNKS, CHUNK, D_MODEL), f))
    return pl.pallas_call(
        functools.partial(_gla_out_kernel, ntp=ntp, tiles_per_seq=tps),
        grid=(nc // TILE_CHUNKS,),
        in_specs=[*op_specs, *os_specs, tok(GLA_HEADS * DVH), *xspecs,
                  pl.BlockSpec((TILE_CHUNKS, 6, D_MODEL), lambda i: (i, 0, 0)),
                  full(g_norm), full(w_out), full(g_post)],
        out_specs=tok(D_MODEL),
        out_shape=jax.ShapeDtypeStruct((nc, CHUNK, D_MODEL), F32),
        compiler_params=_cparams(("arbitrary",)),
        name="gla_out",
    )(*o_p, *o_s, r, x_p, x_s, modc, g_norm, w_out, g_post)


def _conv_kernel(x_ref, mod_ref, hist0_ref, gpre_ref, w1_ref, b1_ref, wdw_ref, bdw_ref, lng_ref, lnb_ref,
                 w2_ref, b2_ref, gpost_ref, x1_ref, hist_out_ref, ext_scr, shift_scr, *, nseq, ntiles):
    j = pl.program_id(1)
    tt = TILE // nseq

    @pl.when(j == 0)
    def _():
        for s in range(nseq):
            ext_scr[s, 0:HIST, :] = hist0_ref[s]

    m = mod_ref[...]
    h = _rms(x_ref[...], gpre_ref[...]) * (1.0 + m[:, 1:2, :]) + m[:, 0:1, :]
    p = _dot(h.reshape(TILE, D_MODEL).astype(BF16), w1_ref[...]) + b1_ref[...]
    u = p[:, :D_MODEL] * _sigmoid(p[:, D_MODEL:])
    first = HIST - (CONV_WIDTH - 1)
    ys = []
    span = tt + HIST - 8
    for s in range(nseq):
        ext_scr[s, HIST:HIST + tt, :] = u[s * tt:(s + 1) * tt]
        for d in range(1, 8):
            shift_scr[d - 1, :, :] = ext_scr[s, d:d + span, :]
        cols = []
        for c in range(D_MODEL // 128):
            lanes = slice(c * 128, (c + 1) * 128)
            acc = jnp.zeros((tt, 128), F32)
            for tap in range(CONV_WIDTH):
                off = first + tap
                if off % 8 == 0:
                    win = ext_scr[s, off:off + tt, lanes]
                else:
                    win = shift_scr[off % 8 - 1, off - off % 8:off - off % 8 + tt, lanes]
                acc = acc + win * wdw_ref[tap:tap + 1, lanes]
            cols.append(acc)
        ys.append(jnp.concatenate(cols, axis=-1))
        hist_new = ext_scr[s, tt:tt + HIST, :]
        ext_scr[s, 0:HIST, :] = hist_new

        @pl.when(j == ntiles - 1)
        def _():
            hist_out_ref[s] = hist_new
    y = jnp.concatenate(ys, axis=0) + bdw_ref[...]
    yc = y - jnp.mean(y, axis=-1, keepdims=True)
    yn = yc * lax.rsqrt(jnp.mean(yc * yc, axis=-1, keepdims=True) + EPS) * lng_ref[...] + lnb_ref[...]
    act = yn * _sigmoid(yn)
    mix = (_dot(act.astype(BF16), w2_ref[...]) + b2_ref[...]).reshape(TILE_CHUNKS, CHUNK, D_MODEL)
    x1_ref[...] = x_ref[...] + m[:, 2:3, :] * _rms(mix, gpost_ref[...])


def _conv(x, modc, hist0, params, *, chunk0, nseq_total, ntiles, nseq):
    tile0 = chunk0 // TILE_CHUNKS
    if ntiles == 1:
        cidx = lambda s, j: (tile0 + s, 0, 0)
        oidx = lambda s, j: (s, 0, 0)
    else:
        assert nseq == 1
        cidx = lambda s, j: (tile0 + s * ntiles + j, 0, 0)
        oidx = lambda s, j: (s * ntiles + j, 0, 0)
    full = lambda a: pl.BlockSpec(a.shape, lambda s, j: (0,) * a.ndim)
    hist_spec = pl.BlockSpec((nseq, HIST, D_MODEL), lambda s, j: (s, 0, 0))
    nblocks = nseq_total // nseq
    return pl.pallas_call(
        functools.partial(_conv_kernel, nseq=nseq, ntiles=ntiles),
        grid=(nblocks, ntiles),
        in_specs=[pl.BlockSpec((TILE_CHUNKS, CHUNK, D_MODEL), cidx),
                  pl.BlockSpec((TILE_CHUNKS, 6, D_MODEL), cidx),
                  hist_spec] + [full(a) for a in params],
        out_specs=[pl.BlockSpec((TILE_CHUNKS, CHUNK, D_MODEL), oidx), hist_spec],
        out_shape=[jax.ShapeDtypeStruct((nblocks * ntiles * TILE_CHUNKS, CHUNK, D_MODEL), F32),
                   jax.ShapeDtypeStruct((nseq_total, HIST, D_MODEL), F32)],
        scratch_shapes=[pltpu.VMEM((nseq, HIST + TILE // nseq, D_MODEL), F32),
                        pltpu.VMEM((7, TILE // nseq + HIST - 8, D_MODEL), F32)],
        compiler_params=_cparams(("arbitrary", "arbitrary")),
        name="conv",
    )(x, modc, hist0, *params)


def _moe_input(x_ref, mod_ref, g_ref):
    m = mod_ref[...]
    return _rms(x_ref[...], g_ref[...]) * (1.0 + m[:, 4:5, :]) + m[:, 3:4, :]


def _router_kernel(x_ref, mod_ref, g_ref, wrt_ref, br_ref,
                   gate_ref, gpos_ref, ce_ref, cbase_ref, nch_ref, cnt_ref, carry_scr):
    i = pl.program_id(0)

    @pl.when(i == 0)
    def _():
        carry_scr[...] = jnp.zeros_like(carry_scr)

    h2 = _moe_input(x_ref, mod_ref, g_ref).reshape(TILE, D_MODEL)
    h_hi = h2.astype(BF16)
    h_lo = (h2 - h_hi.astype(F32)).astype(BF16)
    w = wrt_ref[...]
    w_hi = w.astype(BF16)
    w_lo = (w - w_hi.astype(F32)).astype(BF16)
    logits = _dot_nt(w_hi, h_hi) + _dot_nt(w_lo, h_hi) + _dot_nt(w_hi, h_lo) + br_ref[...]
    e_iota = lax.broadcasted_iota(jnp.int32, (N_EXPERTS, TILE), 0)
    vals, sels = [], []
    lt = logits
    for _ in range(TOP_K):
        mx = jnp.max(lt, axis=0, keepdims=True)
        ix = jnp.min(jnp.where(lt == mx, e_iota, N_EXPERTS), axis=0, keepdims=True)
        sl = e_iota == ix
        vals.append(mx)
        sels.append(sl)
        lt = jnp.where(sl, -jnp.inf, lt)
    ex = [jnp.exp(v - vals[0]) for v in vals]
    den = ex[0] + ex[1] + ex[2] + ex[3]
    gate_ref[...] = _stack_rows([e / den for e in ex])
    chosen = (sels[0] | sels[1] | sels[2] | sels[3])
    chosen_f = jnp.where(chosen, 1.0, 0.0)
    rr = lax.broadcasted_iota(jnp.int32, (TILE, TILE), 0)
    cc = lax.broadcasted_iota(jnp.int32, (TILE, TILE), 1)
    before = (rr < cc).astype(BF16)
    local = _dot(chosen_f.astype(BF16), before)
    cnt = jnp.sum(chosen_f, axis=1, keepdims=True)
    padded = jnp.ceil(cnt * (1.0 / ROW_CHUNK)) * ROW_CHUNK
    er = lax.broadcasted_iota(jnp.int32, (N_EXPERTS, N_EXPERTS), 0)
    ec = lax.broadcasted_iota(jnp.int32, (N_EXPERTS, N_EXPERTS), 1)
    lower = (ec < er).astype(BF16)
    goff = _dot(lower, jnp.broadcast_to(padded, (N_EXPERTS, 128)).astype(BF16))[:, 0:1]
    gpos = [jnp.sum(jnp.where(sl, goff + local, 0.0), axis=0, keepdims=True) for sl in sels]
    gpos_ref[...] = _stack_rows(gpos).astype(jnp.int32)
    carry = carry_scr[...][:, 0:1]
    crow = lax.broadcasted_iota(jnp.int32, (N_EXPERTS, CHUNK_SLOTS), 1).astype(F32) * ROW_CHUNK
    ce = jnp.sum(jnp.where(goff + padded <= crow, 1.0, 0.0), axis=0, keepdims=True)
    ce = jnp.minimum(ce, N_EXPERTS - 1.0)
    e_f = lax.broadcasted_iota(jnp.int32, (N_EXPERTS, CHUNK_SLOTS), 0).astype(F32)
    cbase = jnp.sum(jnp.where(e_f == ce, carry - goff, 0.0), axis=0, keepdims=True) + crow[0:1]
    ce_ref[0] = ce.astype(jnp.int32)
    cbase_ref[0] = cbase.astype(jnp.int32)
    nch = jnp.sum(padded, axis=0, keepdims=True) * (1.0 / ROW_CHUNK)
    nch_ref[0] = jnp.broadcast_to(nch, (1, 128)).astype(jnp.int32)
    carry_scr[...] = carry_scr[...] + padded
    cnt_ref[...] = carry_scr[...]


def _router(x, modc, g_pre, w_router_t, b_router):
    nc = x.shape[0]
    nt = nc // TILE_CHUNKS
    n = nc * CHUNK
    full = lambda a: pl.BlockSpec(a.shape, lambda i: (0,) * a.ndim)
    tok = pl.BlockSpec((TILE_CHUNKS, CHUNK, D_MODEL), lambda i: (i, 0, 0))
    kt = pl.BlockSpec((TOP_K, TILE), lambda i: (0, i))
    tab = lambda w: pl.BlockSpec((1, 1, w), lambda i: (i, 0, 0))
    return pl.pallas_call(
        _router_kernel,
        grid=(nt,),
        in_specs=[tok, pl.BlockSpec((TILE_CHUNKS, 6, D_MODEL), lambda i: (i, 0, 0)),
                  full(g_pre), full(w_router_t), full(b_router)],
        out_specs=[kt, kt, tab(CHUNK_SLOTS), tab(CHUNK_SLOTS), tab(128),
                   pl.BlockSpec((N_EXPERTS, 128), lambda i: (0, 0))],
        out_shape=[jax.ShapeDtypeStruct((TOP_K, n), F32),
                   jax.ShapeDtypeStruct((TOP_K, n), jnp.int32),
                   jax.ShapeDtypeStruct((nt, 1, CHUNK_SLOTS), jnp.int32),
                   jax.ShapeDtypeStruct((nt, 1, CHUNK_SLOTS), jnp.int32),
                   jax.ShapeDtypeStruct((nt, 1, 128), jnp.int32),
                   jax.ShapeDtypeStruct((N_EXPERTS, 128), F32)],
        scratch_shapes=[pltpu.VMEM((N_EXPERTS, 128), F32)],
        compiler_params=_cparams(("arbitrary",)),
        name="router",
    )(x, modc, g_pre, w_router_t, b_router)


def _row_copy(src, dst, sem):
    return pltpu.make_async_copy(src, dst, sem)


def _dispatch_kernel(cdst_ref, nch_ref, pend_ref, x_ref, mod_ref, g_ref, gpos_ref, rows_ref,
                     grp_scr, zero_scr, sem, zsem):
    i = pl.program_id(0)
    nt = pl.num_programs(0)
    slot = i % 2

    @pl.when(i == 0)
    def _():
        zero_scr[...] = jnp.zeros_like(zero_scr)

        n_blocks = rows_ref.shape[0] // EXPERT_ROWS
        total = pend_ref[N_EXPERTS - 1]

        def block_to_clear(e):
            is_tail = e >= N_EXPERTS
            ee = jnp.minimum(e, N_EXPERTS - 1)
            end = pend_ref[ee]
            start = jnp.where(ee == 0, 0, pend_ref[jnp.maximum(ee - 1, 0)])
            tail_row = total + (e - N_EXPERTS) * EXPERT_ROWS
            row0 = jnp.where(is_tail, tail_row, jnp.maximum(end - EXPERT_ROWS, 0))
            needed = jnp.where(is_tail, tail_row < n_blocks * EXPERT_ROWS, end > start)
            row0 = jnp.minimum(row0, (n_blocks - 1) * EXPERT_ROWS)
            dst = rows_ref.at[pl.ds(pl.multiple_of(row0, EXPERT_ROWS), EXPERT_ROWS)]
            return needed, _row_copy(zero_scr, dst, zsem)

        def clear(e, c):
            needed, cp = block_to_clear(e)

            @pl.when(needed)
            def _():
                cp.start()
            return c

        def settle(e, c):
            needed, cp = block_to_clear(e)

            @pl.when(needed)
            def _():
                cp.wait()
            return c

        n_tail = n_blocks - (TILE * TOP_K * nt) // EXPERT_ROWS
        lax.fori_loop(0, N_EXPERTS + n_tail, clear, 0)
        lax.fori_loop(0, N_EXPERTS + n_tail, settle, 0)

    h = _moe_input(x_ref, mod_ref, g_ref).reshape(TILE, D_MODEL).astype(BF16)
    gp = gpos_ref[...]
    jr = lax.broadcasted_iota(jnp.int32, (GROUP_ROWS, TILE), 0)
    hit = jr == gp[0:1]
    for k in range(1, TOP_K):
        hit = hit | (jr == gp[k:k + 1])
    grp_scr[slot] = _dot(jnp.where(hit, 1.0, 0.0).astype(BF16), h)

    def chunk_copy(tile, sl, c):
        d = cdst_ref[tile * CHUNK_SLOTS + c]
        src = grp_scr.at[sl, pl.ds(pl.multiple_of(c * ROW_CHUNK, ROW_CHUNK), ROW_CHUNK)]
        return _row_copy(src, rows_ref.at[pl.ds(pl.multiple_of(d, ROW_CHUNK), ROW_CHUNK)], sem.at[sl])

    n_own = nch_ref[i]

    def issue(c2, carry):
        c = 2 * c2
        chunk_copy(i, slot, c).start(priority=0)

        @pl.when(c + 1 < n_own)
        def _():
            chunk_copy(i, slot, c + 1).start(priority=1)
        return carry

    lax.fori_loop(0, (n_own + 1) // 2, issue, 0)

    def drain(tile, sl):
        n_all = nch_ref[tile]
        many = WAIT_CHUNKS * ROW_CHUNK
        bulk = _row_copy(grp_scr.at[sl, pl.ds(0, many)], rows_ref.at[pl.ds(0, many)], sem.at[sl])

        def bulk_wait(c, carry):
            bulk.wait()
            return carry

        def single_wait(c, carry):
            chunk_copy(tile, sl, 0).wait()
            return carry

        lax.fori_loop(0, n_all // WAIT_CHUNKS, bulk_wait, 0)
        lax.fori_loop(0, n_all % WAIT_CHUNKS, single_wait, 0)

    @pl.when(i > 0)
    def _():
        drain(i - 1, 1 - slot)

    @pl.when(i == nt - 1)
    def _():
        drain(i, slot)


def _dispatch(chunk_dst, n_chunks, p_ends, x, modc, g_pre, gpos_t, n_rows):
    nc = x.shape[0]
    idx3 = lambda i, cd, nch, pend: (i, 0, 0)
    return pl.pallas_call(
        _dispatch_kernel,
        grid_spec=pltpu.PrefetchScalarGridSpec(
            num_scalar_prefetch=3,
            grid=(nc // TILE_CHUNKS,),
            in_specs=[pl.BlockSpec((TILE_CHUNKS, CHUNK, D_MODEL), idx3),
                      pl.BlockSpec((TILE_CHUNKS, 6, D_MODEL), idx3),
                      pl.BlockSpec(g_pre.shape, lambda i, cd, nch, pend: (0, 0)),
                      pl.BlockSpec((TOP_K, TILE), lambda i, cd, nch, pend: (0, i))],
            out_specs=pl.BlockSpec(memory_space=pl.ANY),
            scratch_shapes=[pltpu.VMEM((2, GROUP_ROWS, D_MODEL), F32), pltpu.VMEM((EXPERT_ROWS, D_MODEL), F32),
                            pltpu.SemaphoreType.DMA((2,)), pltpu.SemaphoreType.DMA],
        ),
        out_shape=jax.ShapeDtypeStruct((n_rows, D_MODEL), F32),
        compiler_params=_cparams(("arbitrary",)),
        name="dispatch",
    )(chunk_dst, n_chunks, p_ends, x, modc, g_pre, gpos_t)


def _experts_kernel(be_ref, nu_ref, run_ref, nxt_ref, x_ref, b1_ref, b2_ref, w1_hbm, w2_hbm, y_ref,
                    w1_f32, w2_f32, w1_scr, w2_scr, sem, *, layer):
    i = pl.program_id(0)
    valid = i < nu_ref[0]
    e = be_ref[i]
    changed = jnp.logical_or(i == 0, e != be_ref[jnp.maximum(i - 1, 0)])
    slot = run_ref[i] % 2
    dff = w2_scr.shape[0]

    def fetch(expert, sl):
        return (_row_copy(w1_hbm.at[layer, expert], w1_f32.at[sl], sem.at[sl, 0]),
                _row_copy(w2_hbm.at[layer, expert], w2_f32.at[sl], sem.at[sl, 1]))

    @pl.when(i == 0)
    def _():
        for cp in fetch(e, 0):
            cp.start()

    @pl.when(jnp.logical_and(valid, changed))
    def _():
        nxt = nxt_ref[i]

        @pl.when(nxt >= 0)
        def _():
            for cp in fetch(nxt, 1 - slot):
                cp.start()

        for cp in fetch(e, slot):
            cp.wait()
        rows = 128

        def cast(c, carry):
            r0 = pl.multiple_of(c * rows, rows)
            w1_scr[pl.ds(r0, rows), :] = w1_f32[slot, pl.ds(r0, rows), :].astype(BF16)
            w2_scr[pl.ds(r0, rows), :] = w2_f32[slot, pl.ds(r0, rows), :].astype(BF16)
            return carry

        lax.fori_loop(0, D_MODEL // rows, cast, 0)

    @pl.when(valid)
    def _():
        h = _dot(x_ref[...].astype(BF16), w1_scr[...]) + b1_ref[0, 0]
        gate = jnp.minimum(h[:, :dff], SWIGLU_LIMIT)
        up = jnp.clip(h[:, dff:], -SWIGLU_LIMIT, SWIGLU_LIMIT)
        act = (up + 1.0) * (gate * _sigmoid(SWIGLU_ALPHA * gate))
        y_ref[...] = _dot(act.astype(BF16), w2_scr[...]) + b2_ref[0, 0]

    @pl.when(jnp.logical_not(valid))
    def _():
        y_ref[...] = jnp.zeros_like(y_ref)


def _experts(blk_e, n_used, blk_run, blk_next, x_rows, w1, b1, w2, b2, layer):
    n_rows = x_rows.shape[0]
    n_blocks = n_rows // EXPERT_ROWS
    nl, e, d, f2 = w1.shape
    dff = w2.shape[2]
    assert d == dff == D_MODEL
    row_idx = lambda i, be, nu, run, nxt: (jnp.minimum(i, nu[0] - 1), 0)
    b_idx = lambda i, be, nu, run, nxt: (layer, be[i], 0, 0)
    return pl.pallas_call(
        functools.partial(_experts_kernel, layer=layer),
        grid_spec=pltpu.PrefetchScalarGridSpec(
            num_scalar_prefetch=4,
            grid=(n_blocks,),
            in_specs=[pl.BlockSpec((EXPERT_ROWS, D_MODEL), row_idx),
                      pl.BlockSpec((1, 1, 1, f2), b_idx),
                      pl.BlockSpec((1, 1, 1, D_MODEL), b_idx),
                      pl.BlockSpec(memory_space=pl.ANY),
                      pl.BlockSpec(memory_space=pl.ANY)],
            out_specs=pl.BlockSpec((EXPERT_ROWS, D_MODEL), lambda i, be, nu, run, nxt: (i, 0)),
            scratch_shapes=[pltpu.VMEM((2, d, f2), F32), pltpu.VMEM((2, dff, D_MODEL), F32),
                            pltpu.VMEM((d, f2), BF16), pltpu.VMEM((dff, D_MODEL), BF16),
                            pltpu.SemaphoreType.DMA((2, 2))],
        ),
        out_shape=jax.ShapeDtypeStruct((n_rows, D_MODEL), F32),
        compiler_params=_cparams(("arbitrary",), EXPERTS_VMEM_LIMIT),
        name="experts",
    )(blk_e, n_used, blk_run, blk_next, x_rows, b1.reshape(nl, e, 1, f2), b2.reshape(nl, e, 1, D_MODEL), w1, w2)


def _combine_kernel(cdst_ref, nch_ref, gate_ref, gpos_ref, x_ref, mod_ref, gpost_ref, rows_ref, *rest, ntp):
    o_refs, (ybuf, sem) = rest[:-2], rest[-2:]
    i = pl.program_id(0)
    nt = pl.num_programs(0)
    slot = i % 2

    def chunk_copy(tile, sl, c):
        d = cdst_ref[tile * CHUNK_SLOTS + c]
        dst = ybuf.at[sl, pl.ds(pl.multiple_of(c * ROW_CHUNK, ROW_CHUNK), ROW_CHUNK)]
        return _row_copy(rows_ref.at[pl.ds(pl.multiple_of(d, ROW_CHUNK), ROW_CHUNK)], dst, sem.at[sl])

    def gather_tile(tile, sl):
        n_tile = nch_ref[tile]

        def issue(c2, carry):
            c = 2 * c2
            chunk_copy(tile, sl, c).start(priority=0)

            @pl.when(c + 1 < n_tile)
            def _():
                chunk_copy(tile, sl, c + 1).start(priority=1)
            return carry

        lax.fori_loop(0, (n_tile + 1) // 2, issue, 0)

    @pl.when(i == 0)
    def _():
        ybuf[...] = jnp.zeros_like(ybuf)
        gather_tile(0, 0)

    @pl.when(i + 1 < nt)
    def _():
        gather_tile(i + 1, 1 - slot)

    n_own = nch_ref[i]
    many = WAIT_CHUNKS * ROW_CHUNK
    bulk = _row_copy(rows_ref.at[pl.ds(0, many)], ybuf.at[slot, pl.ds(0, many)], sem.at[slot])

    def bulk_wait(c, carry):
        bulk.wait()
        return carry

    def single_wait(c, carry):
        chunk_copy(i, slot, 0).wait()
        return carry

    lax.fori_loop(0, n_own // WAIT_CHUNKS, bulk_wait, 0)
    lax.fori_loop(0, n_own % WAIT_CHUNKS, single_wait, 0)

    g = gate_ref[...]
    gp = gpos_ref[...]
    jc = lax.broadcasted_iota(jnp.int32, (TILE, GROUP_ROWS), 1)
    gm = jnp.where(jc == gp[:, 0:1], g[:, 0:1], 0.0)
    for k in range(1, TOP_K):
        gm = gm + jnp.where(jc == gp[:, k:k + 1], g[:, k:k + 1], 0.0)
    gm_hi = gm.astype(BF16)
    gm_lo = (gm - gm_hi.astype(F32)).astype(BF16)
    yb = ybuf[slot].astype(BF16)
    y = _dot(gm_hi, yb) + _dot(gm_lo, yb)
    m = mod_ref[...]
    y = y.reshape(TILE_CHUNKS, CHUNK, D_MODEL)
    out = x_ref[...] + m[:, 5:6, :] * _rms(y, gpost_ref[...])
    if ntp is None:
        o_refs[0][...] = out
    else:
        @pl.when(i < ntp)
        def _():
            o_refs[0][...] = out

        @pl.when(i >= ntp)
        def _():
            o_refs[1][...] = out


def _combine(chunk_dst, n_chunks, gates, gpos, x, modc, g_post, y_rows, split_chunks=None):
    nc = x.shape[0]
    ntp = None if split_chunks is None else split_chunks // TILE_CHUNKS
    idx3 = lambda i, cd, nch: (i, 0, 0)
    tok = pl.BlockSpec((TILE_CHUNKS, CHUNK, D_MODEL), idx3)
    per_tok = pl.BlockSpec((TILE, TOP_K), lambda i, cd, nch: (i, 0))
    if ntp is None:
        out_specs, out_shape = tok, jax.ShapeDtypeStruct(x.shape, F32)
    else:
        out_specs = list(_two_part_specs(ntp, lambda f: pl.BlockSpec((TILE_CHUNKS, CHUNK, D_MODEL), f)))
        out_shape = [jax.ShapeDtypeStruct((split_chunks, CHUNK, D_MODEL), F32),
                     jax.ShapeDtypeStruct((nc - split_chunks, CHUNK, D_MODEL), F32)]
    return pl.pallas_call(
        functools.partial(_combine_kernel, ntp=ntp),
        grid_spec=pltpu.PrefetchScalarGridSpec(
            num_scalar_prefetch=2,
            grid=(nc // TILE_CHUNKS,),
            in_specs=[per_tok, per_tok, tok,
                      pl.BlockSpec((TILE_CHUNKS, 6, D_MODEL), idx3),
                      pl.BlockSpec(g_post.shape, lambda i, cd, nch: (0, 0)),
                      pl.BlockSpec(memory_space=pl.ANY)],
            out_specs=out_specs,
            scratch_shapes=[pltpu.VMEM((2, GROUP_ROWS, D_MODEL), F32), pltpu.SemaphoreType.DMA((2,))],
        ),
        out_shape=out_shape,
        compiler_params=_cparams(("arbitrary",)),
        name="combine",
    )(chunk_dst, n_chunks, gates, gpos, x, modc, g_post, y_rows)


def _moe_layer(x, modc, g_pre, g_post, w_router, b_router, w1, b1, w2, b2, layer, split_chunks=None):
    nc = x.shape[0]
    n = nc * CHUNK
    nt = n // TILE
    gate_t, gpos_t, chunk_e, chunk_base, n_chunks, cnt = _router(
        x, modc, g_pre, w_router.T, b_router.reshape(N_EXPERTS, 1))
    counts = cnt[:, 0].astype(jnp.int32)
    padded = (counts + EXPERT_ROWS - 1) // EXPERT_ROWS * EXPERT_ROWS
    p_ends = jnp.cumsum(padded).astype(jnp.int32)
    p_starts = p_ends - padded
    max_rows = n * TOP_K + nt * N_EXPERTS * (ROW_CHUNK - 1)
    n_blocks = -(-max_rows // EXPERT_ROWS) + N_EXPERTS
    n_used = p_ends[-1] // EXPERT_ROWS
    blk = jnp.minimum(jnp.arange(n_blocks, dtype=jnp.int32), n_used - 1) * EXPERT_ROWS
    blk_e = jnp.minimum(jnp.sum((p_ends[None, :] <= blk[:, None]).astype(jnp.int32), axis=1), N_EXPERTS - 1)
    eids = jnp.arange(N_EXPERTS, dtype=jnp.int32)
    chunk_e = chunk_e.reshape(nt, CHUNK_SLOTS)
    chunk_dst = jnp.sum(jnp.where(chunk_e[:, :, None] == eids, p_starts, 0), axis=-1) + chunk_base.reshape(nt, CHUNK_SLOTS)
    chunk_dst = chunk_dst.reshape(-1)
    n_chunks = n_chunks[:, 0, 0]
    blk_run = jnp.cumsum(jnp.concatenate([jnp.ones((1,), jnp.int32),
                                          (blk_e[1:] != blk_e[:-1]).astype(jnp.int32)])) - 1
    later_used = (eids[None, :] > eids[:, None]) & (counts > 0)[None, :]
    next_used = jnp.min(jnp.where(later_used, eids[None, :], N_EXPERTS), axis=1)
    next_used = jnp.where(next_used == N_EXPERTS, -1, next_used)
    blk_next = jnp.sum(jnp.where(blk_e[:, None] == eids, next_used, 0), axis=1)
    x_rows = _dispatch(chunk_dst, n_chunks, p_ends, x, modc, g_pre, gpos_t, n_blocks * EXPERT_ROWS)
    y_rows = _experts(blk_e, n_used.reshape(1), blk_run, blk_next, x_rows, w1, b1, w2, b2, layer)
    return _combine(chunk_dst, n_chunks, gate_t.T, gpos_t.T, x, modc, g_post, y_rows, split_chunks)


def kernel(x_prompt, x_sample, state_gla, cache_conv, c_prompt, c_sample, w_ada, b_ada, norm_pre, norm_post, w_gla_in, w_gla_gate_up, b_gla_gate, g_gla_norm, w_gla_out, w_pw1, b_pw1, w_dw, b_dw, ln_g, ln_b, w_pw2, b_pw2, w_router, b_router, w_e1, b_e1, w_e2, b_e2):
    bp, lp, d = x_prompt.shape
    bs, ls, _ = x_sample.shape
    depth = w_ada.shape[0]
    assert d == D_MODEL and ls == CHUNK and lp % TILE == 0 and bs % TILE_CHUNKS == 0
    assert bp == GLA_SEQS and bs % GLA_SEQS == 0
    ncp = bp * lp // CHUNK
    ncs = bs
    nseq = bp + bs
    nch = lp // CHUNK

    x_p0 = x_prompt.reshape(ncp, CHUNK, d)
    x_s0 = x_sample.reshape(ncs, CHUNK, d)
    x = None
    rows = -(-nseq // 8) * 8
    c_all = jnp.concatenate([c_prompt, c_sample, jnp.zeros((rows - nseq, d), F32)], axis=0)
    mod = _ada(c_all, w_ada, b_ada)
    mod_p = jnp.broadcast_to(mod[:, :bp, None, :], (depth, bp, nch, 6 * d)).reshape(depth, ncp, 6 * d)
    modc = jnp.concatenate([mod_p, mod[:, bp:nseq]], axis=1).reshape(depth, ncp + ncs, 6, d)

    row = lambda a: a.reshape(1, -1)
    dk = GLA_HEADS * DKH
    dv = GLA_HEADS * DVH
    gla_states, conv_hists = [], []
    for i in range(depth):
        j = i // 2
        if i % 2 == 0:
            w_in = w_gla_in[j]
            rank = w_gla_gate_up.shape[1]
            w_main = w_in[:, :2 * dk + 2 * dv].astype(BF16)
            w_gz = jnp.pad(w_in[:, 2 * dk + 2 * dv:], ((0, 0), (0, 128 - rank))).astype(BF16)
            w_gu = jnp.pad(w_gla_gate_up[j], ((0, 128 - rank), (0, 0))).astype(BF16)
            assert i == 0
            q, k, v, r, lg = _gla_in(x_p0, x_s0, modc[i], row(norm_pre[i, 0]), w_main, w_gz, w_gu, row(b_gla_gate[j]))
            s0_p = jnp.zeros((bp, GLA_HEADS, DVH, DKH), F32)
            s0_s = jnp.swapaxes(state_gla[j], -1, -2)
            o_p, st_p = _gla_core(q, k, v, lg, s0_p, chunk0=0, nblocks=1, nchunks=nch)
            o_s, st_s = _gla_core(q, k, v, lg, s0_s, chunk0=ncp, nblocks=bs // GLA_SEQS, nchunks=1)
            gla_states.append((jnp.swapaxes(st_p, -1, -2), jnp.swapaxes(st_s, -1, -2)))
            x = _gla_out(o_p, o_s, r, x_p0, x_s0, modc[i], row(g_gla_norm[j]), w_gla_out[j].astype(BF16),
                         row(norm_post[i, 0]))
        else:
            params = (row(norm_pre[i, 0]), w_pw1[j].astype(BF16), row(b_pw1[j]), w_dw[j], row(b_dw[j]),
                      row(ln_g[j]), row(ln_b[j]), w_pw2[j].astype(BF16), row(b_pw2[j]), row(norm_post[i, 0]))
            keep = CONV_WIDTH - 1
            h0_p = jnp.zeros((bp, HIST, d), F32)
            h0_s = jnp.pad(cache_conv[j], ((0, 0), (HIST - keep, 0), (0, 0)))
            x_p, hp = _conv(x, modc[i], h0_p, params, chunk0=0, nseq_total=bp, ntiles=lp // TILE, nseq=1)
            x_s, hs = _conv(x, modc[i], h0_s, params, chunk0=ncp, nseq_total=bs, ntiles=1, nseq=TILE_CHUNKS)
            x = jnp.concatenate([x_p, x_s], axis=0)
            conv_hists.append((hp[:, HIST - keep:], hs[:, HIST - keep:]))
        x = _moe_layer(x, modc[i], row(norm_pre[i, 1]), row(norm_post[i, 1]),
                       w_router[i], b_router[i], w_e1, b_e1, w_e2, b_e2, i,
                       split_chunks=ncp if i == depth - 1 else None)

    y_prompt = x[0].reshape(bp, lp, d)
    y_sample = x[1].reshape(bs, ls, d)
    gla_p = jnp.stack([s[0] for s in gla_states])
    gla_s = jnp.stack([s[1] for s in gla_states])
    conv_p = jnp.stack([c[0] for c in conv_hists])
    conv_s = jnp.stack([c[1] for c in conv_hists])
    return (y_prompt, y_sample, gla_p, gla_s, conv_p, conv_s)
```
